```python
import math
import jax
import jax.numpy as jnp
from jax import lax
import numpy as np

D_MODEL = 4096
BATCH = 2
SEQ = 8192
DEPTH = 2

HEAD_DIM = 128
D_MIX = D_MODEL
D_CONF = D_MIX // 4
D_SCONV = D_MIX // 4
D_HYENA = D_MIX - D_CONF - D_SCONV
D_IN = 2 * D_CONF + 3 * D_SCONV + 3 * D_HYENA
CONF_WIDTH = 31
SHORT_WIDTH = 3
HYENA_ORDER = 2
HYENA_EMB_DIM = 33
HYENA_HIDDEN = 64
HYENA_FAST_DECAY_PCT = 0.3
HYENA_SLOW_DECAY_PCT = 1.5
HYENA_DECAY_TARGET = 1e-2
D_FF = 11008
FFN_CONV_WIDTH = 3
LN_EPS = 1e-5
DEEPNORM_ALPHA = (2.0 * DEPTH) ** 0.25
DEEPNORM_BETA = (8.0 * DEPTH) ** -0.25

kernel_name = "hybrid_conformer_shortconv_hyena_encoder"


def _layer_norm(x, g, b):
    xf = x.astype(jnp.float32)
    mu = jnp.mean(xf, axis=-1, keepdims=True)
    var = jnp.mean(jnp.square(xf - mu), axis=-1, keepdims=True)
    y = (xf - mu) * lax.rsqrt(var + LN_EPS) * g.astype(jnp.float32) + b.astype(jnp.float32)
    return y.astype(x.dtype)


def _dwconv(x, w):
    k = w.shape[0]
    return lax.conv_general_dilated(
        x, w[:, None, :].astype(x.dtype), window_strides=(1,),
        padding=[(k // 2, k // 2)], dimension_numbers=('NWC', 'WIO', 'NWC'),
        feature_group_count=x.shape[-1])


def _hyena_pos_features(seq_len):
    t = jnp.linspace(0.0, 1.0, seq_len, dtype=jnp.float32)[:, None]
    bands = (HYENA_EMB_DIM - 1) // 2
    w = 2.0 * math.pi * jnp.arange(seq_len, dtype=jnp.float32)[:, None] / seq_len
    f = jnp.linspace(1e-4, bands - 1, bands, dtype=jnp.float32)[None, :]
    z = jnp.concatenate([t, jnp.cos(f * w), -jnp.sin(f * w)], axis=-1)
    return t, z


def _hyena_filters_freq(t, z, w1, b1, w2, b2, w3, freq, decay):
    f32 = jnp.float32
    freq = freq.astype(f32)
    h = jnp.sin(freq * (z @ w1.astype(f32) + b1.astype(f32)))
    h = jnp.sin(freq * (h @ w2.astype(f32) + b2.astype(f32)))
    h = h @ w3.astype(f32)
    seq_len = z.shape[0]
    h = h.reshape(seq_len, HYENA_ORDER, 2, D_HYENA)
    h = h * jnp.exp(-t[:, :, None, None] * jnp.abs(decay.astype(f32)))
    fwd, bwd = h[:, :, 0], h[:, :, 1]
    taps = jnp.concatenate([fwd, jnp.zeros_like(fwd[:1]), bwd[:0:-1]], axis=0)
    return jnp.fft.rfft(taps, axis=0)


def _bidir_long_conv(u, filt_f, d_skip):
    seq_len = u.shape[1]
    u32 = u.astype(jnp.float32)
    uf = jnp.fft.rfft(u32, n=2 * seq_len, axis=1)
    y = jnp.fft.irfft(uf * filt_f[None], n=2 * seq_len, axis=1)[:, :seq_len]
    return (y + u32 * d_skip.astype(jnp.float32)).astype(u.dtype)


def _token_mixer(h, w_in, conf_dw_w, conf_dw_b, conf_ln_g, conf_ln_b, sconv_w,
                 hyena_short_w, filt_f, hyena_d, w_out):
    u = h @ w_in
    s1 = 2 * D_CONF
    s2 = s1 + 3 * D_SCONV
    a_in, b_in, c_in = u[..., :s1], u[..., s1:s2], u[..., s2:]
    a_val, a_gate = jnp.split(a_in, 2, axis=-1)
    a = _dwconv(a_val * jax.nn.sigmoid(a_gate), conf_dw_w) + conf_dw_b
    a = jax.nn.silu(_layer_norm(a, conf_ln_g, conf_ln_b))
    gate_b, gate_c, val = jnp.split(b_in, 3, axis=-1)
    b = gate_b * _dwconv(gate_c * val, sconv_w)
    v, x1, x2 = jnp.split(_dwconv(c_in, hyena_short_w), 3, axis=-1)
    zc = x1 * _bidir_long_conv(v, filt_f[:, 0], hyena_d[0])
    zc = x2 * _bidir_long_conv(zc, filt_f[:, 1], hyena_d[1])
    return jnp.concatenate([a, b, zc], axis=-1) @ w_out


def _conv_ffn(h, w_up, ffn_dw_w, w_down):
    up = _dwconv(h @ w_up, ffn_dw_w)
    g, v = jnp.split(up, 2, axis=-1)
    return (jax.nn.silu(g) * v) @ w_down


def setup_inputs(seed: int = 0) -> dict:
    key = jax.random.key(seed)
    ks = iter(jax.random.split(key, 32))
    f32 = jnp.float32

    def nrm(shape, scale):
        return jax.random.normal(next(ks), shape, f32) * scale

    min_decay = math.log(HYENA_DECAY_TARGET) / HYENA_SLOW_DECAY_PCT
    max_decay = math.log(HYENA_DECAY_TARGET) / HYENA_FAST_DECAY_PCT
    decay_base = jnp.linspace(min_decay, max_decay, D_HYENA, dtype=f32)
    return {
        "x": nrm((BATCH, SEQ, D_MODEL), 1.0),
        "ln_emb_g": 1.0 + nrm((D_MODEL,), 0.02),
        "ln_emb_b": nrm((D_MODEL,), 0.02),
        "w_in": nrm((DEPTH, D_MODEL, D_IN), D_MODEL ** -0.5),
        "conf_dw_w": nrm((DEPTH, CONF_WIDTH, D_CONF), CONF_WIDTH ** -0.5),
        "conf_dw_b": nrm((DEPTH, D_CONF), 0.02),
        "conf_ln_g": 1.0 + nrm((DEPTH, D_CONF), 0.02),
        "conf_ln_b": nrm((DEPTH, D_CONF), 0.02),
        "sconv_w": nrm((DEPTH, SHORT_WIDTH, D_SCONV), SHORT_WIDTH ** -0.5),
        "hyena_short_w": nrm((DEPTH, SHORT_WIDTH, 3 * D_HYENA), SHORT_WIDTH ** -0.5),
        "hyena_w1": nrm((DEPTH, HYENA_EMB_DIM, HYENA_HIDDEN), HYENA_EMB_DIM ** -0.5),
        "hyena_b1": nrm((DEPTH, HYENA_HIDDEN), 0.1),
        "hyena_w2": nrm((DEPTH, HYENA_HIDDEN, HYENA_HIDDEN), HYENA_HIDDEN ** -0.5),
        "hyena_b2": nrm((DEPTH, HYENA_HIDDEN), 0.1),
        "hyena_w3": nrm((DEPTH, HYENA_HIDDEN, HYENA_ORDER * 2 * D_HYENA), 0.05 * HYENA_HIDDEN ** -0.5),
        "hyena_freq": 1.0 + nrm((DEPTH, HYENA_HIDDEN), 0.02),
        "hyena_decay": decay_base + nrm((DEPTH, HYENA_ORDER, 2, D_HYENA), 0.01),
        "hyena_d": nrm((DEPTH, HYENA_ORDER, D_HYENA), 1.0),
        "w_out": nrm((DEPTH, D_MIX, D_MODEL), DEEPNORM_BETA * D_MIX ** -0.5),
        "ln1_g": 1.0 + nrm((DEPTH, D_MODEL), 0.02),
        "ln1_b": nrm((DEPTH, D_MODEL), 0.02),
        "w_up": nrm((DEPTH, D_MODEL, 2 * D_FF), D_MODEL ** -0.5),
        "ffn_dw_w": nrm((DEPTH, FFN_CONV_WIDTH, 2 * D_FF), FFN_CONV_WIDTH ** -0.5),
        "w_down": nrm((DEPTH, D_FF, D_MODEL), DEEPNORM_BETA * D_FF ** -0.5),
        "ln2_g": 1.0 + nrm((DEPTH, D_MODEL), 0.02),
        "ln2_b": nrm((DEPTH, D_MODEL), 0.02),
    }


def reference(x, ln_emb_g, ln_emb_b, w_in, conf_dw_w, conf_dw_b, conf_ln_g, conf_ln_b,
              sconv_w, hyena_short_w, hyena_w1, hyena_b1, hyena_w2, hyena_b2, hyena_w3,
              hyena_freq, hyena_decay, hyena_d, w_out, ln1_g, ln1_b,
              w_up, ffn_dw_w, w_down, ln2_g, ln2_b):
    t, z = _hyena_pos_features(x.shape[1])
    h = _layer_norm(x, ln_emb_g, ln_emb_b)
    for l in range(DEPTH):
        filt_f = _hyena_filters_freq(t, z, hyena_w1[l], hyena_b1[l], hyena_w2[l], hyena_b2[l],
                                     hyena_w3[l], hyena_freq[l], hyena_decay[l])
        mix = _token_mixer(h, w_in[l], conf_dw_w[l], conf_dw_b[l], conf_ln_g[l], conf_ln_b[l],
                           sconv_w[l], hyena_short_w[l], filt_f, hyena_d[l], w_out[l])
        h = _layer_norm(DEEPNORM_ALPHA * h + mix, ln1_g[l], ln1_b[l])
        ffn = _conv_ffn(h, w_up[l], ffn_dw_w[l], w_down[l])
        h = _layer_norm(DEEPNORM_ALPHA * h + ffn, ln2_g[l], ln2_b[l])
    return h
```

```python
import functools
import math

import jax
import jax.numpy as jnp
from jax import lax
from jax.experimental import pallas as pl
from jax.experimental.pallas import tpu as pltpu

F32 = jnp.float32
BF16 = jnp.bfloat16

LN_EPS = 1e-5
CONF_WIDTH = 31
SHORT_WIDTH = 3
HALO = 16
V7X_VMEM_BYTES = 64 * 1024 * 1024
VMEM_CAP = V7X_VMEM_BYTES - 8 * 1024 * 1024


def _vmem_limit(block_bytes, scratch_bytes=0):
    est = 2 * sum(block_bytes) + scratch_bytes
    return int(min(VMEM_CAP, max(32 * 1024 * 1024, 2 * est)))


def _nbytes(shape, dtype):
    return math.prod(shape) * jnp.dtype(dtype).itemsize


def _ln_rows(y, g, b):
    mu = jnp.mean(y, axis=-1, keepdims=True)
    d = y - mu
    var = jnp.mean(d * d, axis=-1, keepdims=True)
    return d * lax.rsqrt(var + LN_EPS) * g + b


def _ln_kernel(x_ref, g_ref, b_ref, o_ref, obf_ref):
    y = _ln_rows(x_ref[...], g_ref[...], b_ref[...])
    o_ref[...] = y
    obf_ref[...] = y.astype(BF16)


def _res_ln_kernel(alpha, h_ref, y_ref, g_ref, b_ref, o_ref, obf_ref):
    y = _ln_rows(alpha * h_ref[...] + y_ref[...], g_ref[...], b_ref[...])
    o_ref[...] = y
    obf_ref[...] = y.astype(BF16)


def _layer_norm(x, g, b, residual=None, alpha=1.0, tr=256):
    t, d = x.shape
    row = pl.BlockSpec((tr, d), lambda i: (i, 0))
    vec = pl.BlockSpec((1, d), lambda i: (0, 0))
    outs = (jax.ShapeDtypeStruct((t, d), F32), jax.ShapeDtypeStruct((t, d), BF16))
    blk = _nbytes((tr, d), F32)
    if residual is None:
        kern, ins, specs, nblk = _ln_kernel, (x,), [row], 3
    else:
        kern = functools.partial(_res_ln_kernel, alpha)
        ins, specs, nblk = (residual, x), [row, row], 4
    return pl.pallas_call(
        kern, grid=(t // tr,), in_specs=specs + [vec, vec], out_specs=(row, row), out_shape=outs,
        compiler_params=pltpu.CompilerParams(
            dimension_semantics=("parallel",), vmem_limit_bytes=_vmem_limit([blk] * nblk)),
    )(*ins, g.reshape(1, d), b.reshape(1, d))


def _mm_kernel(a_ref, b_ref, o_ref):
    o_ref[...] = jnp.dot(a_ref[...], b_ref[...], preferred_element_type=F32).astype(o_ref.dtype)


def _matmul(a, b, out_dtype, tm=1024, tn=1024):
    m, k = a.shape
    _, n = b.shape
    blocks = [_nbytes((tm, k), a.dtype), _nbytes((k, tn), b.dtype), _nbytes((tm, tn), out_dtype)]
    return pl.pallas_call(
        _mm_kernel, grid=(m // tm, n // tn),
        in_specs=[pl.BlockSpec((tm, k), lambda i, j: (i, 0)), pl.BlockSpec((k, tn), lambda i, j: (0, j))],
        out_specs=pl.BlockSpec((tm, tn), lambda i, j: (i, j)),
        out_shape=jax.ShapeDtypeStruct((m, n), out_dtype),
        compiler_params=pltpu.CompilerParams(
            dimension_semantics=("parallel", "parallel"),
            vmem_limit_bytes=_vmem_limit(blocks, _nbytes((tm, tn), F32))),
    )(a, b)


def _mm_ksplit_kernel(a_ref, b_ref, o_ref):
    @pl.when(pl.program_id(2) == 0)
    def _():
        o_ref[...] = jnp.zeros_like(o_ref)
    o_ref[...] += jnp.dot(a_ref[...], b_ref[...], preferred_element_type=F32)


def _matmul_ksplit(a, b, tm=1024, tn=1024, tk=2816):
    m, k = a.shape
    _, n = b.shape
    blocks = [_nbytes((tm, tk), a.dtype), _nbytes((tk, tn), b.dtype), _nbytes((tm, tn), F32)]
    return pl.pallas_call(
        _mm_ksplit_kernel, grid=(m // tm, n // tn, k // tk),
        in_specs=[pl.BlockSpec((tm, tk), lambda i, j, kk: (i, kk)),
                  pl.BlockSpec((tk, tn), lambda i, j, kk: (kk, j))],
        out_specs=pl.BlockSpec((tm, tn), lambda i, j, kk: (i, j)),
        out_shape=jax.ShapeDtypeStruct((m, n), F32),
        compiler_params=pltpu.CompilerParams(
            dimension_semantics=("parallel", "parallel", "arbitrary"),
            vmem_limit_bytes=_vmem_limit(blocks, _nbytes((tm, tn), F32))),
    )(a, b)


def _halo_specs(seq_len, tr, cb, col_of):
    r = tr // HALO
    last = seq_len // HALO - 1

    def cur(b, i, *rest):
        return (b, i, col_of(*rest))

    def prev(b, i, *rest):
        return (b, jnp.maximum(i * r - 1, 0), col_of(*rest))

    def nxt(b, i, *rest):
        return (b, jnp.minimum((i + 1) * r, last), col_of(*rest))

    return [pl.BlockSpec((None, HALO, cb), prev), pl.BlockSpec((None, tr, cb), cur),
            pl.BlockSpec((None, HALO, cb), nxt)]


def _fill_ext(ext_ref, prev, cur, nxt):
    i = pl.program_id(1)
    tr = cur.shape[0]
    ext_ref[0:HALO, :] = jnp.where(i > 0, prev, 0.0)
    ext_ref[HALO:HALO + tr, :] = cur
    ext_ref[HALO + tr:2 * HALO + tr, :] = jnp.where(i < pl.num_programs(1) - 1, nxt, 0.0)


def _dwconv_tile(ext_ref, w_ref, r0, rc, c0, cc, width):
    acc = None
    for k in range(width):
        x = ext_ref[pl.ds(HALO - width // 2 + k + r0, rc), c0:c0 + cc]
        term = w_ref[k:k + 1, c0:c0 + cc] * x
        acc = term if acc is None else acc + term
    return acc


def _conf_kernel(vp, vc, vn, gp, gc, gn, w_ref, bias_ref, lg_ref, lb_ref, o_ref, ext_ref, conv_ref):
    def glu(v, g):
        return v[...] * jax.nn.sigmoid(g[...])

    _fill_ext(ext_ref, glu(vp, gp), glu(vc, gc), glu(vn, gn))
    tr, c = conv_ref.shape
    for c0 in range(0, c, 128):
        for r0 in range(0, tr, 64):
            conv_ref[r0:r0 + 64, c0:c0 + 128] = _dwconv_tile(ext_ref, w_ref, r0, 64, c0, 128, CONF_WIDTH)
    for r0 in range(0, tr, 16):
        y = _ln_rows(conv_ref[r0:r0 + 16, :] + bias_ref[...], lg_ref[...], lb_ref[...])
        o_ref[r0:r0 + 16, :] = (y * jax.nn.sigmoid(y)).astype(o_ref.dtype)


def _conformer_branch(u, w, bias, ln_g, ln_b, col0, dc, tr=128):
    b, l, _ = u.shape
    specs = _halo_specs(l, tr, dc, lambda: col0) + _halo_specs(l, tr, dc, lambda: col0 + 1)
    vec = pl.BlockSpec((1, dc), lambda bb, i: (0, 0))
    blocks = [_nbytes((tr + 2 * HALO, dc), F32)] * 2 + [_nbytes((CONF_WIDTH, dc), F32), _nbytes((tr, dc), BF16)]
    scratch = _nbytes((tr + 2 * HALO, dc), F32) + _nbytes((tr, dc), F32)
    return pl.pallas_call(
        _conf_kernel, grid=(b, l // tr),
        in_specs=specs + [pl.BlockSpec((CONF_WIDTH, dc), lambda bb, i: (0, 0)), vec, vec, vec],
        out_specs=pl.BlockSpec((None, tr, dc), lambda bb, i: (bb, i, 0)),
        out_shape=jax.ShapeDtypeStruct((b, l, dc), BF16),
        scratch_shapes=[pltpu.VMEM((tr + 2 * HALO, dc), F32), pltpu.VMEM((tr, dc), F32)],
        compiler_params=pltpu.CompilerParams(
            dimension_semantics=("parallel", "parallel"), vmem_limit_bytes=_vmem_limit(blocks, scratch)),
    )(u, u, u, u, u, u, w, bias.reshape(1, dc), ln_g.reshape(1, dc), ln_b.reshape(1, dc))


def _sconv_kernel(gb_ref, cp, cc, cn, xp, xc, xn, w_ref, o_ref, ext_ref):
    _fill_ext(ext_ref, cp[...] * xp[...], cc[...] * xc[...], cn[...] * xn[...])
    tr, c = o_ref.shape
    for c0 in range(0, c, 256):
        for r0 in range(0, tr, 32):
            y = _dwconv_tile(ext_ref, w_ref, r0, 32, c0, 256, SHORT_WIDTH)
            o_ref[r0:r0 + 32, c0:c0 + 256] = (gb_ref[r0:r0 + 32, c0:c0 + 256] * y).astype(o_ref.dtype)


def _sconv_branch(u, w, col0, ds, tr=256):
    b, l, _ = u.shape
    specs = ([pl.BlockSpec((None, tr, ds), lambda bb, i: (bb, i, col0))]
             + _halo_specs(l, tr, ds, lambda: col0 + 1) + _halo_specs(l, tr, ds, lambda: col0 + 2))
    blocks = [_nbytes((tr + 2 * HALO, ds), F32)] * 3 + [_nbytes((tr, ds), BF16)]
    return pl.pallas_call(
        _sconv_kernel, grid=(b, l // tr),
        in_specs=specs + [pl.BlockSpec((SHORT_WIDTH, ds), lambda bb, i: (0, 0))],
        out_specs=pl.BlockSpec((None, tr, ds), lambda bb, i: (bb, i, 0)),
        out_shape=jax.ShapeDtypeStruct((b, l, ds), BF16),
        scratch_shapes=[pltpu.VMEM((tr + 2 * HALO, ds), F32)],
        compiler_params=pltpu.CompilerParams(
            dimension_semantics=("parallel", "parallel"),
            vmem_limit_bytes=_vmem_limit(blocks, _nbytes((tr + 2 * HALO, ds), F32))),
    )(u, u, u, u, u, u, u, w)


def _hyena_short_kernel(xp, xc, xn, w_ref, v_ref, x1_ref, x2_ref, ext_ref):
    _fill_ext(ext_ref, xp[...], xc[...], xn[...])
    tr, dh = v_ref.shape
    for part, o_ref in enumerate((v_ref, x1_ref, x2_ref)):
        for c0 in range(0, dh, 256):
            for r0 in range(0, tr, 32):
                o_ref[r0:r0 + 32, c0:c0 + 256] = _dwconv_tile(
                    ext_ref, w_ref, r0, 32, part * dh + c0, 256, SHORT_WIDTH)


def _hyena_short_conv(u, w, col0, dh, tr=128):
    b, l, _ = u.shape
    out = jax.ShapeDtypeStruct((b, l, dh), F32)
    ospec = pl.BlockSpec((None, tr, dh), lambda bb, i: (bb, i, 0))
    blocks = [_nbytes((tr + 2 * HALO, 3 * dh), F32)] + [_nbytes((tr, dh), F32)] * 3
    return pl.pallas_call(
        _hyena_short_kernel, grid=(b, l // tr),
        in_specs=(_halo_specs(l, tr, 3 * dh, lambda: col0)
                  + [pl.BlockSpec((SHORT_WIDTH, 3 * dh), lambda bb, i: (0, 0))]),
        out_specs=(ospec, ospec, ospec), out_shape=(out, out, out),
        scratch_shapes=[pltpu.VMEM((tr + 2 * HALO, 3 * dh), F32)],
        compiler_params=pltpu.CompilerParams(
            dimension_semantics=("parallel", "parallel"),
            vmem_limit_bytes=_vmem_limit(blocks, _nbytes((tr + 2 * HALO, 3 * dh), F32))),
    )(u, u, u, w)


def _ffn_gate_kernel(gp, gc, gn, vp, vc, vn, wg_ref, wv_ref, o_ref, eg_ref, ev_ref):
    f = lambda r: r[...].astype(F32)
    _fill_ext(eg_ref, f(gp), f(gc), f(gn))
    _fill_ext(ev_ref, f(vp), f(vc), f(vn))
    tr, c = o_ref.shape
    for c0 in range(0, c, 256):
        for r0 in range(0, tr, 32):
            g = _dwconv_tile(eg_ref, wg_ref, r0, 32, c0, 256, SHORT_WIDTH)
            v = _dwconv_tile(ev_ref, wv_ref, r0, 32, c0, 256, SHORT_WIDTH)
            o_ref[r0:r0 + 32, c0:c0 + 256] = (g * jax.nn.sigmoid(g) * v).astype(o_ref.dtype)


def _ffn_gate(up, w, dffp, tr=512, cb=1024):
    b, l, _ = up.shape
    nc = dffp // cb
    specs = _halo_specs(l, tr, cb, lambda j: j) + _halo_specs(l, tr, cb, lambda j: j + nc)
    blocks = [_nbytes((tr + 2 * HALO, cb), BF16)] * 2 + [_nbytes((tr, cb), BF16)]
    scratch = 2 * _nbytes((tr + 2 * HALO, cb), F32)
    return pl.pallas_call(
        _ffn_gate_kernel, grid=(b, l // tr, nc),
        in_specs=specs + [pl.BlockSpec((SHORT_WIDTH, cb), lambda bb, i, j: (0, j)),
                          pl.BlockSpec((SHORT_WIDTH, cb), lambda bb, i, j: (0, j + nc))],
        out_specs=pl.BlockSpec((None, tr, cb), lambda bb, i, j: (bb, i, j)),
        out_shape=jax.ShapeDtypeStruct((b, l, dffp), BF16),
        scratch_shapes=[pltpu.VMEM((tr + 2 * HALO, cb), F32)] * 2,
        compiler_params=pltpu.CompilerParams(
            dimension_semantics=("parallel", "parallel", "parallel"),
            vmem_limit_bytes=_vmem_limit(blocks, scratch)),
    )(up, up, up, up, up, up, w, w)


def _dft_constants(n1):
    n = n1 * n1
    h = n1 // 2
    idx = jnp.arange(n1, dtype=jnp.int32)
    ang1 = (2.0 * math.pi / n1) * ((idx[:, None] * idx[None, :]) % n1).astype(F32)
    c1, s1 = jnp.cos(ang1), jnp.sin(ang1)
    m1 = jnp.stack([jnp.concatenate([c1[:, :h], s1[:, :h]], axis=1),
                    jnp.concatenate([-s1[:, :h], c1[:, :h]], axis=1)], axis=1).reshape(2 * n1, n1)
    m1f = jnp.stack([c1, -s1], axis=1).reshape(2 * n1, n1)
    ct, st = c1[:h, :], s1[:h, :]
    m3 = jnp.concatenate([jnp.stack([ct, -st], axis=2).reshape(h, 2 * n1),
                          jnp.stack([st, ct], axis=2).reshape(h, 2 * n1)], axis=0) / n
    k = idx[:, None, None] + n1 * idx[None, :, None]
    ang2 = (2.0 * math.pi / n) * ((k * idx[None, None, :]) % n).astype(F32)
    c2, s2 = jnp.cos(ang2), jnp.sin(ang2)
    g = jnp.concatenate([jnp.concatenate([c2, s2], axis=2), jnp.concatenate([-s2, c2], axis=2)], axis=1)
    gt = jnp.swapaxes(g, 1, 2)
    return tuple(m.astype(BF16) for m in (m1, m1f, g, gt, m3))


def _stage1_kernel(m_ref, x_ref, o_ref):
    o_ref[...] = jnp.dot(m_ref[...], x_ref[...].astype(BF16), preferred_element_type=F32).astype(o_ref.dtype)


def _dft_stage1(m, x, wblk):
    r, k = m.shape
    _, w = x.shape
    blocks = [_nbytes((k, wblk), x.dtype), _nbytes((r, wblk), BF16), _nbytes((r, k), BF16)]
    return pl.pallas_call(
        _stage1_kernel, grid=(w // wblk,),
        in_specs=[pl.BlockSpec((r, k), lambda j: (0, 0)), pl.BlockSpec((k, wblk), lambda j: (0, j))],
        out_specs=pl.BlockSpec((r, wblk), lambda j: (0, j)),
        out_shape=jax.ShapeDtypeStruct((r, w), BF16),
        compiler_params=pltpu.CompilerParams(
            dimension_semantics=("parallel",), vmem_limit_bytes=_vmem_limit(blocks, _nbytes((r, wblk), F32))),
    )(m, x)


def _filter_stage2_kernel(g_ref, t_ref, o_ref):
    two, n1, c = t_ref.shape
    x = jnp.dot(g_ref[...], t_ref[...].reshape(two * n1, c), preferred_element_type=F32)
    o_ref[...] = x.reshape(two, n1, c)


def _filter_stage2(g, t):
    n1, _, _, c = t.shape
    blocks = [_nbytes((2 * n1, 2 * n1), BF16), _nbytes((2, n1, c), BF16), _nbytes((2, n1, c), F32)]
    return pl.pallas_call(
        _filter_stage2_kernel, grid=(n1,),
        in_specs=[pl.BlockSpec((None, 2 * n1, 2 * n1), lambda k: (k, 0, 0)),
                  pl.BlockSpec((None, 2, n1, c), lambda k: (k, 0, 0, 0))],
        out_specs=pl.BlockSpec((None, 2, n1, c), lambda k: (k, 0, 0, 0)),
        out_shape=jax.ShapeDtypeStruct((n1, 2, n1, c), F32),
        compiler_params=pltpu.CompilerParams(
            dimension_semantics=("parallel",), vmem_limit_bytes=_vmem_limit(blocks, blocks[2])),
    )(g, t)


def _stage2_kernel(g_ref, gt_ref, z_ref, h_ref, o_ref):
    two, n1, c = z_ref.shape
    x = jnp.dot(g_ref[...], z_ref[...].reshape(two * n1, c), preferred_element_type=F32)
    xr, xi = x[:n1], x[n1:]
    hr, hi = h_ref[0], h_ref[1]
    y = jnp.concatenate([xr * hr - xi * hi, xr * hi + xi * hr], axis=0).astype(BF16)
    o_ref[...] = jnp.dot(gt_ref[...], y, preferred_element_type=F32).astype(o_ref.dtype).reshape(two, n1, c)


def _dft_stage2(g, gt, z, hf, order):
    n1, _, _, c = z.shape
    blocks = ([_nbytes((2 * n1, 2 * n1), BF16)] * 2
              + [_nbytes((2, n1, c), BF16), _nbytes((2, n1, c), F32), _nbytes((2, n1, c), BF16)])
    return pl.pallas_call(
        _stage2_kernel, grid=(n1,),
        in_specs=[pl.BlockSpec((None, 2 * n1, 2 * n1), lambda k: (k, 0, 0)),
                  pl.BlockSpec((None, 2 * n1, 2 * n1), lambda k: (k, 0, 0)),
                  pl.BlockSpec((None, 2, n1, c), lambda k: (k, 0, 0, 0)),
                  pl.BlockSpec((None, 2, n1, c), lambda k: (k, 0, 0, order))],
        out_specs=pl.BlockSpec((None, 2, n1, c), lambda k: (k, 0, 0, 0)),
        out_shape=jax.ShapeDtypeStruct((n1, 2, n1, c), BF16),
        compiler_params=pltpu.CompilerParams(
            dimension_semantics=("parallel",), vmem_limit_bytes=_vmem_limit(blocks, 4 * blocks[3])),
    )(g, gt, z, hf)


def _stage3_kernel(m_ref, b_ref, u_ref, x_ref, d_ref, o_ref):
    y = jnp.dot(m_ref[...], b_ref[...], preferred_element_type=F32)
    o_ref[...] = (x_ref[...] * (y + u_ref[...] * d_ref[...])).astype(o_ref.dtype)


def _dft_stage3(m3, bz, u, xg, d, out_dtype, reps=2):
    r, k = m3.shape
    _, w = u.shape
    c = d.shape[0]
    wblk = reps * c
    dt = jnp.tile(d.reshape(1, c), (1, reps))
    blocks = [_nbytes((k, wblk), BF16), _nbytes((r, wblk), F32) * 2, _nbytes((r, wblk), out_dtype)]
    return pl.pallas_call(
        _stage3_kernel, grid=(w // wblk,),
        in_specs=[pl.BlockSpec((r, k), lambda j: (0, 0)), pl.BlockSpec((k, wblk), lambda j: (0, j)),
                  pl.BlockSpec((r, wblk), lambda j: (0, j)), pl.BlockSpec((r, wblk), lambda j: (0, j)),
                  pl.BlockSpec((1, wblk), lambda j: (0, 0))],
        out_specs=pl.BlockSpec((r, wblk), lambda j: (0, j)),
        out_shape=jax.ShapeDtypeStruct((r, w), out_dtype),
        compiler_params=pltpu.CompilerParams(
            dimension_semantics=("parallel",), vmem_limit_bytes=_vmem_limit(blocks, _nbytes((r, wblk), F32))),
    )(m3, bz, u, xg, dt)


def _long_conv_gate(consts, u, xg, hf, order, d, out_dtype):
    m1, _, g, gt, m3 = consts
    b, l, c = u.shape
    n1 = m1.shape[1]
    u2 = u.reshape(n1, n1 * c)
    z = _dft_stage1(m1, u2, 4 * c).reshape(n1, 2, n1, c)
    bz = _dft_stage2(g, gt, z, hf, order).reshape(2 * n1, n1 * c)
    out = _dft_stage3(m3, bz, u2, xg.reshape(n1, n1 * c), d, out_dtype)
    return out.reshape(b, l, c)


def _filter_mlp_kernel(seq_len, z_ref, t_ref, w1_ref, b1_ref, w2_ref, b2_ref, fr_ref, w3_ref, dec_ref, o_ref):
    hp = lax.Precision.HIGHEST
    fr = fr_ref[...]
    h = jnp.sin(fr * (jnp.dot(z_ref[...], w1_ref[...], precision=hp, preferred_element_type=F32) + b1_ref[...]))
    h = jnp.sin(fr * (jnp.dot(h, w2_ref[...], precision=hp, preferred_element_type=F32) + b2_ref[...]))
    h = jnp.dot(h, w3_ref[...], precision=hp, preferred_element_type=F32)
    h = h * jnp.exp(-t_ref[...] * dec_ref[...])
    tf = o_ref.shape[0]
    row = pl.program_id(0) * tf + lax.broadcasted_iota(jnp.int32, (tf, 1), 0)
    o_ref[...] = jnp.where(row == seq_len, 0.0, h).astype(o_ref.dtype)


def _hyena_taps(z_ext, t_ext, w1, b1, w2, b2, w3, freq, decay, tf=512):
    n2l, e = z_ext.shape
    seq_len = n2l // 2
    hid = w1.shape[1]
    order, _, c = decay.shape
    w3d = w3.reshape(hid, order, 2, c).transpose(2, 0, 1, 3).reshape(2, hid, order * c)
    dec = jnp.abs(decay).transpose(1, 0, 2).reshape(2, 1, order * c)
    half = seq_len // tf
    full = lambda shape: pl.BlockSpec(shape, lambda i: (0,) * len(shape))
    blocks = [_nbytes((tf, 128), F32) * 2, _nbytes((hid, order * c), F32), _nbytes((tf, order * c), BF16)]
    return pl.pallas_call(
        functools.partial(_filter_mlp_kernel, seq_len), grid=(n2l // tf,),
        in_specs=[pl.BlockSpec((tf, e), lambda i: (i, 0)), pl.BlockSpec((tf, 1), lambda i: (i, 0)),
                  full((e, hid)), full((1, hid)), full((hid, hid)), full((1, hid)), full((1, hid)),
                  pl.BlockSpec((None, hid, order * c), lambda i: (i // half, 0, 0)),
                  pl.BlockSpec((None, 1, order * c), lambda i: (i // half, 0, 0))],
        out_specs=pl.BlockSpec((tf, order * c), lambda i: (i, 0)),
        out_shape=jax.ShapeDtypeStruct((n2l, order * c), BF16),
        compiler_params=pltpu.CompilerParams(
            dimension_semantics=("parallel",),
            vmem_limit_bytes=_vmem_limit(blocks, 3 * _nbytes((tf, order * c), F32))),
    )(z_ext, t_ext, w1, b1.reshape(1, hid), w2, b2.reshape(1, hid), freq.reshape(1, hid), w3d, dec)


def _pos_features(seq_len, emb_dim):
    t = jnp.linspace(0.0, 1.0, seq_len, dtype=F32)[:, None]
    bands = (emb_dim - 1) // 2
    w = 2.0 * math.pi * jnp.arange(seq_len, dtype=F32)[:, None] / seq_len
    f = jnp.linspace(1e-4, bands - 1, bands, dtype=F32)[None, :]
    z = jnp.concatenate([t, jnp.cos(f * w), -jnp.sin(f * w)], axis=-1)
    pad = (-emb_dim) % 8
    z = jnp.pad(z, ((0, 0), (0, pad)))
    ext = lambda a: jnp.concatenate([a, a[:1], a[:0:-1]], axis=0)
    return ext(z), ext(t), pad


def kernel(x, ln_emb_g, ln_emb_b, w_in, conf_dw_w, conf_dw_b, conf_ln_g, conf_ln_b, sconv_w, hyena_short_w, hyena_w1, hyena_b1, hyena_w2, hyena_b2, hyena_w3, hyena_freq, hyena_decay, hyena_d, w_out, ln1_g, ln1_b, w_up, ffn_dw_w, w_down, ln2_g, ln2_b):
    b, l, d = x.shape
    depth = w_in.shape[0]
    dc = conf_dw_w.shape[2]
    ds = sconv_w.shape[2]
    dh = hyena_d.shape[2]
    dff = w_down.shape[1]
    dffp = -(-dff // 1024) * 1024
    alpha = (2.0 * depth) ** 0.25
    t = b * l
    n1 = math.isqrt(2 * l)
    assert b == 2 and n1 * n1 == 2 * l, "long conv packs two batch rows into one complex 2L = n1*n1 transform"

    consts = _dft_constants(n1)
    z_ext, t_ext, epad = _pos_features(l, hyena_w1.shape[1])

    h32, h16 = _layer_norm(x.reshape(t, d), ln_emb_g, ln_emb_b)
    for i in range(depth):
        taps = _hyena_taps(z_ext, t_ext, jnp.pad(hyena_w1[i], ((0, epad), (0, 0))), hyena_b1[i], hyena_w2[i],
                           hyena_b2[i], hyena_w3[i], hyena_freq[i], hyena_decay[i])
        oc = taps.shape[1]
        tf = _dft_stage1(consts[1], taps.reshape(n1, n1 * oc), 2 * oc).reshape(n1, 2, n1, oc)
        hf = _filter_stage2(consts[2], tf)

        split = 2 * dc + 3 * ds
        wi = jnp.concatenate([w_in[i, :, split:], w_in[i, :, :split]], axis=1).astype(BF16)
        u = _matmul(h16, wi, F32).reshape(b, l, -1)
        a = _conformer_branch(u, conf_dw_w[i], conf_dw_b[i], conf_ln_g[i], conf_ln_b[i], 3 * dh // dc, dc)
        bb = _sconv_branch(u, sconv_w[i], (3 * dh + 2 * dc) // ds, ds)
        v, x1, x2 = _hyena_short_conv(u, hyena_short_w[i], 0, dh)
        zc = _long_conv_gate(consts, v, x1, hf, 0, hyena_d[i, 0], F32)
        zc = _long_conv_gate(consts, zc, x2, hf, 1, hyena_d[i, 1], BF16)
        cat = jnp.concatenate([a, bb, zc], axis=-1).reshape(t, -1)
        mix = _matmul(cat, w_out[i].astype(BF16), F32)
        h32, h16 = _layer_norm(mix, ln1_g[i], ln1_b[i], residual=h32, alpha=alpha)

        pad = dffp - dff
        wu = w_up[i].astype(BF16).reshape(d, 2, dff)
        wu = jnp.pad(wu, ((0, 0), (0, 0), (0, pad))).reshape(d, 2 * dffp)
        wc = jnp.pad(ffn_dw_w[i].reshape(-1, 2, dff), ((0, 0), (0, 0), (0, pad))).reshape(-1, 2 * dffp)
        wd = jnp.pad(w_down[i].astype(BF16), ((0, pad), (0, 0)))
        up = _matmul(h16, wu, BF16).reshape(b, l, 2 * dffp)
        gated = _ffn_gate(up, wc, dffp).reshape(t, dffp)
        ffn = _matmul_ksplit(gated, wd, tk=dffp // 4)
        h32, h16 = _layer_norm(ffn, ln2_g[i], ln2_b[i], residual=h32, alpha=alpha)
    return h32.reshape(b, l, d)
```

```python
import functools
import math

import jax
import jax.numpy as jnp
from jax import lax
from jax.experimental import pallas as pl
from jax.experimental.pallas import tpu as pltpu

F32 = jnp.float32
BF16 = jnp.bfloat16

LN_EPS = 1e-5
CONF_WIDTH = 31
SHORT_WIDTH = 3
HALO = 16
LANES = 128
SUBLANES = 8
V7X_VMEM_BYTES = 64 * 1024 * 1024
VMEM_CAP = V7X_VMEM_BYTES - 8 * 1024 * 1024


def _vmem_limit(block_bytes, scratch_bytes=0):
    est = 2 * sum(block_bytes) + scratch_bytes
    return int(min(VMEM_CAP, max(32 * 1024 * 1024, 2 * est)))


def _nbytes(shape, dtype):
    return math.prod(shape) * jnp.dtype(dtype).itemsize


def _ln_rows(y, g, b):
    mu = jnp.mean(y, axis=-1, keepdims=True)
    d = y - mu
    var = jnp.mean(d * d, axis=-1, keepdims=True)
    return d * lax.rsqrt(var + LN_EPS) * g + b


def _ln_kernel(x_ref, g_ref, b_ref, o_ref, obf_ref):
    y = _ln_rows(x_ref[...], g_ref[...], b_ref[...])
    o_ref[...] = y
    obf_ref[...] = y.astype(BF16)


def _res_ln_kernel(alpha, h_ref, y_ref, g_ref, b_ref, o_ref, obf_ref):
    y = _ln_rows(alpha * h_ref[...] + y_ref[...], g_ref[...], b_ref[...])
    o_ref[...] = y
    obf_ref[...] = y.astype(BF16)


def _layer_norm(x, g, b, residual=None, alpha=1.0, tr=256):
    t, d = x.shape
    row = pl.BlockSpec((tr, d), lambda i: (i, 0))
    vec = pl.BlockSpec((1, d), lambda i: (0, 0))
    outs = (jax.ShapeDtypeStruct((t, d), F32), jax.ShapeDtypeStruct((t, d), BF16))
    blk = _nbytes((tr, d), F32)
    if residual is None:
        kern, ins, specs, nblk = _ln_kernel, (x,), [row], 3
    else:
        kern = functools.partial(_res_ln_kernel, alpha)
        ins, specs, nblk = (residual, x), [row, row], 4
    return pl.pallas_call(
        kern, grid=(t // tr,), in_specs=specs + [vec, vec], out_specs=(row, row), out_shape=outs,
        name="layer_norm" if residual is None else "residual_layer_norm",
        compiler_params=pltpu.CompilerParams(
            dimension_semantics=("parallel",), vmem_limit_bytes=_vmem_limit([blk] * nblk)),
    )(*ins, g.reshape(1, d), b.reshape(1, d))


def _mm_kernel(a_ref, b_ref, o_ref):
    o_ref[...] = jnp.dot(a_ref[...], b_ref[...], preferred_element_type=F32).astype(o_ref.dtype)


def _matmul(a, b, out_dtype, name, tm=1024, tn=1024):
    m, k = a.shape
    _, n = b.shape
    blocks = [_nbytes((tm, k), a.dtype), _nbytes((k, tn), b.dtype), _nbytes((tm, tn), out_dtype)]
    return pl.pallas_call(
        _mm_kernel, grid=(m // tm, n // tn),
        in_specs=[pl.BlockSpec((tm, k), lambda i, j: (i, 0)), pl.BlockSpec((k, tn), lambda i, j: (0, j))],
        out_specs=pl.BlockSpec((tm, tn), lambda i, j: (i, j)),
        out_shape=jax.ShapeDtypeStruct((m, n), out_dtype), name=name,
        compiler_params=pltpu.CompilerParams(
            dimension_semantics=("parallel", "parallel"),
            vmem_limit_bytes=_vmem_limit(blocks, _nbytes((tm, tn), F32))),
    )(a, b)


def _mm_ksplit_kernel(a_ref, b_ref, o_ref):
    @pl.when(pl.program_id(2) == 0)
    def _():
        o_ref[...] = jnp.zeros_like(o_ref)
    o_ref[...] += jnp.dot(a_ref[...], b_ref[...], preferred_element_type=F32)


def _matmul_ksplit(a, b, name, tm=1024, tn=1024, tk=2816):
    m, k = a.shape
    _, n = b.shape
    blocks = [_nbytes((tm, tk), a.dtype), _nbytes((tk, tn), b.dtype), _nbytes((tm, tn), F32)]
    return pl.pallas_call(
        _mm_ksplit_kernel, grid=(m // tm, n // tn, k // tk),
        in_specs=[pl.BlockSpec((tm, tk), lambda i, j, kk: (i, kk)),
                  pl.BlockSpec((tk, tn), lambda i, j, kk: (kk, j))],
        out_specs=pl.BlockSpec((tm, tn), lambda i, j, kk: (i, j)),
        out_shape=jax.ShapeDtypeStruct((m, n), F32), name=name,
        compiler_params=pltpu.CompilerParams(
            dimension_semantics=("parallel", "parallel", "arbitrary"),
            vmem_limit_bytes=_vmem_limit(blocks, _nbytes((tm, tn), F32))),
    )(a, b)


def _halo_specs(seq_len, tr, cb, col_of):
    r = tr // HALO
    last = seq_len // HALO - 1

    def cur(b, i, *rest):
        return (b, i, col_of(*rest))

    def prev(b, i, *rest):
        return (b, jnp.maximum(i * r - 1, 0), col_of(*rest))

    def nxt(b, i, *rest):
        return (b, jnp.minimum((i + 1) * r, last), col_of(*rest))

    return [pl.BlockSpec((None, HALO, cb), prev), pl.BlockSpec((None, tr, cb), cur),
            pl.BlockSpec((None, HALO, cb), nxt)]


def _fill_ext(ext_ref, prev, cur, nxt):
    i = pl.program_id(1)
    tr = cur.shape[0]
    ext_ref[0:HALO, :] = jnp.where(i > 0, prev, 0.0)
    ext_ref[HALO:HALO + tr, :] = cur
    ext_ref[HALO + tr:2 * HALO + tr, :] = jnp.where(i < pl.num_programs(1) - 1, nxt, 0.0)


def _dwconv_tile(ext_ref, w_ref, r0, rc, c0, cc, width):
    acc = None
    for k in range(width):
        x = ext_ref[pl.ds(HALO - width // 2 + k + r0, rc), c0:c0 + cc]
        term = w_ref[k:k + 1, c0:c0 + cc] * x
        acc = term if acc is None else acc + term
    return acc


def _conf_kernel(vp, vc, vn, gp, gc, gn, w_ref, bias_ref, lg_ref, lb_ref, o_ref, ext_ref, conv_ref):
    def glu(v, g):
        return v[...] * jax.nn.sigmoid(g[...])

    _fill_ext(ext_ref, glu(vp, gp), glu(vc, gc), glu(vn, gn))
    tr, c = conv_ref.shape
    for c0 in range(0, c, 128):
        for r0 in range(0, tr, 64):
            conv_ref[r0:r0 + 64, c0:c0 + 128] = _dwconv_tile(ext_ref, w_ref, r0, 64, c0, 128, CONF_WIDTH)
    for r0 in range(0, tr, 16):
        y = _ln_rows(conv_ref[r0:r0 + 16, :] + bias_ref[...], lg_ref[...], lb_ref[...])
        o_ref[r0:r0 + 16, :] = (y * jax.nn.sigmoid(y)).astype(o_ref.dtype)


def _conformer_branch(u, w, bias, ln_g, ln_b, col0, dc, tr=128):
    b, l, _ = u.shape
    specs = _halo_specs(l, tr, dc, lambda: col0) + _halo_specs(l, tr, dc, lambda: col0 + 1)
    vec = pl.BlockSpec((1, dc), lambda bb, i: (0, 0))
    blocks = [_nbytes((tr + 2 * HALO, dc), F32)] * 2 + [_nbytes((CONF_WIDTH, dc), F32), _nbytes((tr, dc), BF16)]
    scratch = _nbytes((tr + 2 * HALO, dc), F32) + _nbytes((tr, dc), F32)
    return pl.pallas_call(
        _conf_kernel, grid=(b, l // tr),
        in_specs=specs + [pl.BlockSpec((CONF_WIDTH, dc), lambda bb, i: (0, 0)), vec, vec, vec],
        out_specs=pl.BlockSpec((None, tr, dc), lambda bb, i: (bb, i, 0)),
        out_shape=jax.ShapeDtypeStruct((b, l, dc), BF16), name="conformer_branch",
        scratch_shapes=[pltpu.VMEM((tr + 2 * HALO, dc), F32), pltpu.VMEM((tr, dc), F32)],
        compiler_params=pltpu.CompilerParams(
            dimension_semantics=("parallel", "parallel"), vmem_limit_bytes=_vmem_limit(blocks, scratch)),
    )(u, u, u, u, u, u, w, bias.reshape(1, dc), ln_g.reshape(1, dc), ln_b.reshape(1, dc))


def _sconv_kernel(gb_ref, cp, cc, cn, xp, xc, xn, w_ref, o_ref, ext_ref):
    _fill_ext(ext_ref, cp[...] * xp[...], cc[...] * xc[...], cn[...] * xn[...])
    tr, c = o_ref.shape
    for c0 in range(0, c, 256):
        for r0 in range(0, tr, 32):
            y = _dwconv_tile(ext_ref, w_ref, r0, 32, c0, 256, SHORT_WIDTH)
            o_ref[r0:r0 + 32, c0:c0 + 256] = (gb_ref[r0:r0 + 32, c0:c0 + 256] * y).astype(o_ref.dtype)


def _sconv_branch(u, w, col0, ds, tr=256):
    b, l, _ = u.shape
    specs = ([pl.BlockSpec((None, tr, ds), lambda bb, i: (bb, i, col0))]
             + _halo_specs(l, tr, ds, lambda: col0 + 1) + _halo_specs(l, tr, ds, lambda: col0 + 2))
    blocks = [_nbytes((tr + 2 * HALO, ds), F32)] * 3 + [_nbytes((tr, ds), BF16)]
    return pl.pallas_call(
        _sconv_kernel, grid=(b, l // tr),
        in_specs=specs + [pl.BlockSpec((SHORT_WIDTH, ds), lambda bb, i: (0, 0))],
        out_specs=pl.BlockSpec((None, tr, ds), lambda bb, i: (bb, i, 0)),
        out_shape=jax.ShapeDtypeStruct((b, l, ds), BF16), name="sconv_branch",
        scratch_shapes=[pltpu.VMEM((tr + 2 * HALO, ds), F32)],
        compiler_params=pltpu.CompilerParams(
            dimension_semantics=("parallel", "parallel"),
            vmem_limit_bytes=_vmem_limit(blocks, _nbytes((tr + 2 * HALO, ds), F32))),
    )(u, u, u, u, u, u, u, w)


def _hyena_short_kernel(xp, xc, xn, w_ref, v_ref, x1_ref, x2_ref, ext_ref):
    _fill_ext(ext_ref, xp[...], xc[...], xn[...])
    tr, dh = v_ref.shape
    for part, o_ref in enumerate((v_ref, x1_ref, x2_ref)):
        for c0 in range(0, dh, 256):
            for r0 in range(0, tr, 32):
                o_ref[r0:r0 + 32, c0:c0 + 256] = _dwconv_tile(
                    ext_ref, w_ref, r0, 32, part * dh + c0, 256, SHORT_WIDTH)


def _hyena_short_conv(u, w, col0, dh, tr=128):
    b, l, _ = u.shape
    out = jax.ShapeDtypeStruct((b, l, dh), F32)
    ospec = pl.BlockSpec((None, tr, dh), lambda bb, i: (bb, i, 0))
    blocks = [_nbytes((tr + 2 * HALO, 3 * dh), F32)] + [_nbytes((tr, dh), F32)] * 3
    return pl.pallas_call(
        _hyena_short_kernel, grid=(b, l // tr),
        in_specs=(_halo_specs(l, tr, 3 * dh, lambda: col0)
                  + [pl.BlockSpec((SHORT_WIDTH, 3 * dh), lambda bb, i: (0, 0))]),
        out_specs=(ospec, ospec, ospec), out_shape=(out, out, out), name="hyena_short_conv",
        scratch_shapes=[pltpu.VMEM((tr + 2 * HALO, 3 * dh), F32)],
        compiler_params=pltpu.CompilerParams(
            dimension_semantics=("parallel", "parallel"),
            vmem_limit_bytes=_vmem_limit(blocks, _nbytes((tr + 2 * HALO, 3 * dh), F32))),
    )(u, u, u, w)


MXU_COLS = 256
GATE_ROWS = 256


def _ffn_up_gate_kernel(tiles_per_seq, ap_ref, ac_ref, an_ref, wg_ref, wv_ref, cg_ref, cv_ref, o_ref, a_ext,
                        up_ref):
    tm, tn = o_ref.shape
    rows = tm + 2 * HALO

    @pl.when(pl.program_id(1) == 0)
    def _():
        t = pl.program_id(0) % tiles_per_seq
        a_ext[0:HALO, :] = jnp.where(t > 0, ap_ref[...], jnp.zeros_like(ap_ref))
        a_ext[HALO:HALO + tm, :] = ac_ref[...]
        a_ext[HALO + tm:, :] = jnp.where(t < tiles_per_seq - 1, an_ref[...], jnp.zeros_like(an_ref))

    def conv(x, c_ref, c0):
        n = x.shape[0]
        w = c_ref[:, c0:c0 + MXU_COLS]
        y = w[0:1] * pltpu.roll(x, 1, 0) + w[1:2] * x + w[2:3] * pltpu.roll(x, n - 1, 0)
        return y[SUBLANES:n - SUBLANES]

    for s, c0 in enumerate(range(0, tn, MXU_COLS)):
        a = a_ext[...]
        up_ref[0, s] = jnp.dot(a, wg_ref[:, c0:c0 + MXU_COLS], preferred_element_type=F32)
        up_ref[1, s] = jnp.dot(a, wv_ref[:, c0:c0 + MXU_COLS], preferred_element_type=F32)
        for r0 in range(0, tm, GATE_ROWS):
            lo, hi = HALO + r0 - SUBLANES, HALO + r0 + GATE_ROWS + SUBLANES
            g = conv(up_ref[0, s, lo:hi, :], cg_ref, c0)
            v = conv(up_ref[1, s, lo:hi, :], cv_ref, c0)
            hg = 0.5 * g
            o_ref[r0:r0 + GATE_ROWS, c0:c0 + MXU_COLS] = ((hg + hg * jnp.tanh(hg)) * v).astype(o_ref.dtype)


def _ffn_up_gate(h, wu, wc, seq_len, tm=1024, tn=512):
    t, d = h.shape
    f = wu.shape[1] // 2
    nc = f // tn
    r = tm // HALO
    last = t // HALO - 1
    specs = [pl.BlockSpec((HALO, d), lambda i, j: (jnp.maximum(i * r - 1, 0), 0)),
             pl.BlockSpec((tm, d), lambda i, j: (i, 0)),
             pl.BlockSpec((HALO, d), lambda i, j: (jnp.minimum((i + 1) * r, last), 0)),
             pl.BlockSpec((d, tn), lambda i, j: (0, j)), pl.BlockSpec((d, tn), lambda i, j: (0, j + nc)),
             pl.BlockSpec((SHORT_WIDTH, tn), lambda i, j: (0, j)),
             pl.BlockSpec((SHORT_WIDTH, tn), lambda i, j: (0, j + nc))]
    blocks = [_nbytes((tm + 2 * HALO, d), BF16), 2 * _nbytes((d, tn), BF16), _nbytes((tm, tn), BF16)]
    scratch = _nbytes((tm + 2 * HALO, d), BF16) + 4 * _nbytes((tm + 2 * HALO, MXU_COLS), F32)
    return pl.pallas_call(
        functools.partial(_ffn_up_gate_kernel, seq_len // tm), grid=(t // tm, nc),
        in_specs=specs, out_specs=pl.BlockSpec((tm, tn), lambda i, j: (i, j)),
        out_shape=jax.ShapeDtypeStruct((t, f), BF16), name="ffn_up_gate",
        scratch_shapes=[pltpu.VMEM((tm + 2 * HALO, d), BF16),
                        pltpu.VMEM((2, tn // MXU_COLS, tm + 2 * HALO, MXU_COLS), F32)],
        compiler_params=pltpu.CompilerParams(
            dimension_semantics=("parallel", "arbitrary"), vmem_limit_bytes=_vmem_limit(blocks, scratch)),
    )(h, h, h, wu, wu, wc, wc)


def _dft_constants(n1):
    n = n1 * n1
    h = n1 // 2
    idx = jnp.arange(n1, dtype=jnp.int32)
    ang1 = (2.0 * math.pi / n1) * ((idx[:, None] * idx[None, :]) % n1).astype(F32)
    c1, s1 = jnp.cos(ang1), jnp.sin(ang1)
    m1 = jnp.stack([jnp.concatenate([c1[:, :h], s1[:, :h]], axis=1),
                    jnp.concatenate([-s1[:, :h], c1[:, :h]], axis=1)], axis=1).reshape(2 * n1, n1)
    m1f = jnp.stack([c1, -s1], axis=1).reshape(2 * n1, n1)
    ct, st = c1[:h, :], s1[:h, :]
    m3 = jnp.concatenate([jnp.stack([ct, -st], axis=2).reshape(h, 2 * n1),
                          jnp.stack([st, ct], axis=2).reshape(h, 2 * n1)], axis=0) / n
    k = idx[:, None, None] + n1 * idx[None, :, None]
    ang2 = (2.0 * math.pi / n) * ((k * idx[None, None, :]) % n).astype(F32)
    c2, s2 = jnp.cos(ang2), jnp.sin(ang2)
    g = jnp.concatenate([jnp.concatenate([c2, s2], axis=2), jnp.concatenate([-s2, c2], axis=2)], axis=1)
    gt = jnp.swapaxes(g, 1, 2)
    return tuple(m.astype(BF16) for m in (m1, m1f, g, gt, m3))


def _gather_rows(ref, count):
    flat = ref.reshape(count * SUBLANES, LANES)
    return jnp.concatenate([flat[pl.ds(j, count, stride=SUBLANES), :] for j in range(SUBLANES)], axis=1)


def _scatter_rows(ref, val):
    count = val.shape[0]
    flat = ref.reshape(count * SUBLANES, LANES)
    for j in range(SUBLANES):
        flat[pl.ds(j, count, stride=SUBLANES), :] = val[:, j * LANES:(j + 1) * LANES]


def _stage1_kernel(m_ref, x_ref, o_ref):
    x = _gather_rows(x_ref, x_ref.shape[0]).astype(BF16)
    _scatter_rows(o_ref, jnp.dot(m_ref[...], x, preferred_element_type=F32))


def _dft_stage1(m, x, name):
    r, k = m.shape
    _, n2, c = x.shape
    blocks = [_nbytes((k, SUBLANES, LANES), F32), _nbytes((r, SUBLANES, LANES), F32), _nbytes((r, k), BF16)]
    return pl.pallas_call(
        _stage1_kernel, grid=(c // LANES, n2 // SUBLANES),
        in_specs=[pl.BlockSpec((r, k), lambda ct, jb: (0, 0)),
                  pl.BlockSpec((k, SUBLANES, LANES), lambda ct, jb: (0, jb, ct))],
        out_specs=pl.BlockSpec((r, SUBLANES, LANES), lambda ct, jb: (0, jb, ct)),
        out_shape=jax.ShapeDtypeStruct((r, n2, c), F32), name=name,
        compiler_params=pltpu.CompilerParams(
            dimension_semantics=("parallel", "parallel"), vmem_limit_bytes=_vmem_limit(blocks, 4 * blocks[1])),
    )(m, x)


def _filter_stage2_kernel(g_ref, t_ref, o_ref):
    two, n1, c = t_ref.shape
    x = jnp.dot(g_ref[...], t_ref[...].reshape(two * n1, c).astype(BF16), preferred_element_type=F32)
    o_ref[...] = x.reshape(two, n1, c)


def _filter_stage2(g, t):
    n1, _, _, c = t.shape
    blocks = [_nbytes((2 * n1, 2 * n1), BF16), _nbytes((2, n1, c), F32), _nbytes((2, n1, c), F32)]
    return pl.pallas_call(
        _filter_stage2_kernel, grid=(n1,),
        in_specs=[pl.BlockSpec((None, 2 * n1, 2 * n1), lambda k: (k, 0, 0)),
                  pl.BlockSpec((None, 2, n1, c), lambda k: (k, 0, 0, 0))],
        out_specs=pl.BlockSpec((None, 2, n1, c), lambda k: (k, 0, 0, 0)),
        out_shape=jax.ShapeDtypeStruct((n1, 2, n1, c), F32), name="filter_stage2",
        compiler_params=pltpu.CompilerParams(
            dimension_semantics=("parallel",), vmem_limit_bytes=_vmem_limit(blocks, 2 * blocks[2])),
    )(g, t)


def _stage2_kernel(g_ref, gt_ref, z_ref, h_ref, o_ref):
    two, n1, c = z_ref.shape
    x = jnp.dot(g_ref[...], z_ref[...].reshape(two * n1, c).astype(BF16), preferred_element_type=F32)
    xr, xi = x[:n1], x[n1:]
    hr, hi = h_ref[0], h_ref[1]
    y = jnp.concatenate([xr * hr - xi * hi, xr * hi + xi * hr], axis=0).astype(BF16)
    o_ref[...] = jnp.dot(gt_ref[...], y, preferred_element_type=F32).reshape(two, n1, c)


def _dft_stage2(g, gt, z, hf, order):
    n1, _, _, c = z.shape
    blocks = [_nbytes((2 * n1, 2 * n1), BF16)] * 2 + [_nbytes((2, n1, c), F32)] * 3
    return pl.pallas_call(
        _stage2_kernel, grid=(n1,),
        in_specs=[pl.BlockSpec((None, 2 * n1, 2 * n1), lambda k: (k, 0, 0)),
                  pl.BlockSpec((None, 2 * n1, 2 * n1), lambda k: (k, 0, 0)),
                  pl.BlockSpec((None, 2, n1, c), lambda k: (k, 0, 0, 0)),
                  pl.BlockSpec((None, 2, n1, c), lambda k: (k, 0, 0, order))],
        out_specs=pl.BlockSpec((None, 2, n1, c), lambda k: (k, 0, 0, 0)),
        out_shape=jax.ShapeDtypeStruct((n1, 2, n1, c), F32), name="dft_stage2",
        compiler_params=pltpu.CompilerParams(
            dimension_semantics=("parallel",), vmem_limit_bytes=_vmem_limit(blocks, 4 * blocks[3])),
    )(g, gt, z, hf)


def _stage3_kernel(m_ref, b_ref, u_ref, x_ref, d_ref, o_ref, y_ref):
    b = _gather_rows(b_ref, b_ref.shape[0]).astype(BF16)
    _scatter_rows(y_ref, jnp.dot(m_ref[...], b, preferred_element_type=F32))
    o_ref[...] = x_ref[...] * (y_ref[...] + u_ref[...] * d_ref[...])


def _dft_stage3(m3, bz, u, xg, d):
    r, k = m3.shape
    _, n2, c = u.shape
    tile = pl.BlockSpec((r, SUBLANES, LANES), lambda ct, jb: (0, jb, ct))
    blocks = [_nbytes((k, SUBLANES, LANES), F32)] + [_nbytes((r, SUBLANES, LANES), F32)] * 3
    return pl.pallas_call(
        _stage3_kernel, grid=(c // LANES, n2 // SUBLANES),
        in_specs=[pl.BlockSpec((r, k), lambda ct, jb: (0, 0)),
                  pl.BlockSpec((k, SUBLANES, LANES), lambda ct, jb: (0, jb, ct)), tile, tile,
                  pl.BlockSpec((1, 1, LANES), lambda ct, jb: (0, 0, ct))],
        out_specs=tile, out_shape=jax.ShapeDtypeStruct((r, n2, c), F32), name="dft_stage3",
        scratch_shapes=[pltpu.VMEM((r, SUBLANES, LANES), F32)],
        compiler_params=pltpu.CompilerParams(
            dimension_semantics=("parallel", "parallel"), vmem_limit_bytes=_vmem_limit(blocks, 4 * blocks[0])),
    )(m3, bz, u, xg, d.reshape(1, 1, c))


def _long_conv_gate(consts, u, xg, hf, order, d):
    m1, _, g, gt, m3 = consts
    b, l, c = u.shape
    n1 = m1.shape[1]
    u3 = u.reshape(n1, n1, c)
    z = _dft_stage1(m1, u3, "dft_stage1").reshape(n1, 2, n1, c)
    bz = _dft_stage2(g, gt, z, hf, order).reshape(2 * n1, n1, c)
    return _dft_stage3(m3, bz, u3, xg.reshape(n1, n1, c), d).reshape(b, l, c)


def _filter_mlp_kernel(seq_len, z_ref, t_ref, w1_ref, b1_ref, w2_ref, b2_ref, fr_ref, w3_ref, dec_ref, o_ref):
    hp = lax.Precision.HIGHEST
    fr = fr_ref[...]
    h = jnp.sin(fr * (jnp.dot(z_ref[...], w1_ref[...], precision=hp, preferred_element_type=F32) + b1_ref[...]))
    h = jnp.sin(fr * (jnp.dot(h, w2_ref[...], precision=hp, preferred_element_type=F32) + b2_ref[...]))
    h = jnp.dot(h.astype(BF16), w3_ref[...].astype(BF16), preferred_element_type=F32)
    h = h * jnp.exp(-t_ref[...] * dec_ref[...])
    tf = o_ref.shape[0]
    row = pl.program_id(0) * tf + lax.broadcasted_iota(jnp.int32, (tf, 1), 0)
    o_ref[...] = jnp.where(row == seq_len, 0.0, h).astype(o_ref.dtype)


def _hyena_taps(z_ext, t_ext, w1, b1, w2, b2, w3, freq, decay, tf=512):
    n2l, e = z_ext.shape
    seq_len = n2l // 2
    hid = w1.shape[1]
    order, _, c = decay.shape
    w3d = w3.reshape(hid, order, 2, c).transpose(2, 0, 1, 3).reshape(2, hid, order * c)
    dec = jnp.abs(decay).transpose(1, 0, 2).reshape(2, 1, order * c)
    half = seq_len // tf
    full = lambda shape: pl.BlockSpec(shape, lambda i: (0,) * len(shape))
    blocks = [_nbytes((tf, 128), F32) * 2, _nbytes((hid, order * c), F32), _nbytes((tf, order * c), F32)]
    return pl.pallas_call(
        functools.partial(_filter_mlp_kernel, seq_len), grid=(n2l // tf,),
        in_specs=[pl.BlockSpec((tf, e), lambda i: (i, 0)), pl.BlockSpec((tf, 1), lambda i: (i, 0)),
                  full((e, hid)), full((1, hid)), full((hid, hid)), full((1, hid)), full((1, hid)),
                  pl.BlockSpec((None, hid, order * c), lambda i: (i // half, 0, 0)),
                  pl.BlockSpec((None, 1, order * c), lambda i: (i // half, 0, 0))],
        out_specs=pl.BlockSpec((tf, order * c), lambda i: (i, 0)),
        out_shape=jax.ShapeDtypeStruct((n2l, order * c), F32), name="filter_taps",
        compiler_params=pltpu.CompilerParams(
            dimension_semantics=("parallel",),
            vmem_limit_bytes=_vmem_limit(blocks, 3 * _nbytes((tf, order * c), F32))),
    )(z_ext, t_ext, w1, b1.reshape(1, hid), w2, b2.reshape(1, hid), freq.reshape(1, hid), w3d, dec)


def _pos_features(seq_len, emb_dim):
    t = jnp.linspace(0.0, 1.0, seq_len, dtype=F32)[:, None]
    bands = (emb_dim - 1) // 2
    w = 2.0 * math.pi * jnp.arange(seq_len, dtype=F32)[:, None] / seq_len
    f = jnp.linspace(1e-4, bands - 1, bands, dtype=F32)[None, :]
    z = jnp.concatenate([t, jnp.cos(f * w), -jnp.sin(f * w)], axis=-1)
    pad = (-emb_dim) % 8
    z = jnp.pad(z, ((0, 0), (0, pad)))
    ext = lambda a: jnp.concatenate([a, a[:1], a[:0:-1]], axis=0)
    return ext(z), ext(t), pad


def kernel(x, ln_emb_g, ln_emb_b, w_in, conf_dw_w, conf_dw_b, conf_ln_g, conf_ln_b, sconv_w, hyena_short_w, hyena_w1, hyena_b1, hyena_w2, hyena_b2, hyena_w3, hyena_freq, hyena_decay, hyena_d, w_out, ln1_g, ln1_b, w_up, ffn_dw_w, w_down, ln2_g, ln2_b):
    b, l, d = x.shape
    depth = w_in.shape[0]
    dc = conf_dw_w.shape[2]
    ds = sconv_w.shape[2]
    dh = hyena_d.shape[2]
    dff = w_down.shape[1]
    dffp = -(-dff // 1024) * 1024
    alpha = (2.0 * depth) ** 0.25
    t = b * l
    n1 = math.isqrt(2 * l)
    assert b == 2 and n1 * n1 == 2 * l, "long conv packs two batch rows into one complex 2L = n1*n1 transform"

    consts = _dft_constants(n1)
    z_ext, t_ext, epad = _pos_features(l, hyena_w1.shape[1])

    h32, h16 = _layer_norm(x.reshape(t, d), ln_emb_g, ln_emb_b)
    for i in range(depth):
        taps = _hyena_taps(z_ext, t_ext, jnp.pad(hyena_w1[i], ((0, epad), (0, 0))), hyena_b1[i], hyena_w2[i],
                           hyena_b2[i], hyena_w3[i], hyena_freq[i], hyena_decay[i])
        oc = taps.shape[1]
        tf = _dft_stage1(consts[1], taps.reshape(n1, n1, oc), "filter_stage1").reshape(n1, 2, n1, oc)
        hf = _filter_stage2(consts[2], tf)

        split = 2 * dc + 3 * ds
        wi = jnp.concatenate([w_in[i, :, split:], w_in[i, :, :split]], axis=1).astype(BF16)
        u = _matmul(h16, wi, F32, "mm_in").reshape(b, l, -1)
        a = _conformer_branch(u, conf_dw_w[i], conf_dw_b[i], conf_ln_g[i], conf_ln_b[i], 3 * dh // dc, dc)
        bb = _sconv_branch(u, sconv_w[i], (3 * dh + 2 * dc) // ds, ds)
        v, x1, x2 = _hyena_short_conv(u, hyena_short_w[i], 0, dh)
        zc = _long_conv_gate(consts, v, x1, hf, 0, hyena_d[i, 0])
        zc = _long_conv_gate(consts, zc, x2, hf, 1, hyena_d[i, 1])
        cat = jnp.concatenate([a, bb, zc.astype(BF16)], axis=-1).reshape(t, -1)
        mix = _matmul(cat, w_out[i].astype(BF16), F32, "mm_out")
        h32, h16 = _layer_norm(mix, ln1_g[i], ln1_b[i], residual=h32, alpha=alpha)

        pad = dffp - dff
        halves = lambda w: jnp.concatenate(
            [jnp.pad(w[:, :dff], ((0, 0), (0, pad))), jnp.pad(w[:, dff:], ((0, 0), (0, pad)))], axis=1)
        wu = halves(w_up[i].astype(BF16))
        wc = halves(ffn_dw_w[i])
        wd = jnp.pad(w_down[i].astype(BF16), ((0, pad), (0, 0)))
        gated = _ffn_up_gate(h16, wu, wc, l)
        ffn = _matmul_ksplit(gated, wd, "mm_down", tk=dffp // 4)
        h32, h16 = _layer_norm(ffn, ln2_g[i], ln2_b[i], residual=h32, alpha=alpha)
    return h32.reshape(b, l, d)
```

```python
import functools
import math

import jax
import jax.numpy as jnp
from jax import lax
from jax.experimental import pallas as pl
from jax.experimental.pallas import tpu as pltpu

F32 = jnp.float32
BF16 = jnp.bfloat16

LN_EPS = 1e-5
CONF_WIDTH = 31
SHORT_WIDTH = 3
HALO = 16
LANES = 128
SUBLANES = 8
DFT_SLABS = 4
CONV_ROWS = 128
V7X_VMEM_BYTES = 64 * 1024 * 1024
VMEM_CAP = V7X_VMEM_BYTES - 8 * 1024 * 1024


def _vmem_limit(block_bytes, scratch_bytes=0):
    est = 2 * sum(block_bytes) + scratch_bytes
    return int(min(VMEM_CAP, max(32 * 1024 * 1024, 2 * est)))


def _nbytes(shape, dtype):
    return math.prod(shape) * jnp.dtype(dtype).itemsize


def _ln_rows(y, g, b):
    mu = jnp.mean(y, axis=-1, keepdims=True)
    d = y - mu
    var = jnp.mean(d * d, axis=-1, keepdims=True)
    return d * lax.rsqrt(var + LN_EPS) * g + b


def _ln_kernel(x_ref, g_ref, b_ref, o_ref, obf_ref):
    y = _ln_rows(x_ref[...], g_ref[...], b_ref[...])
    o_ref[...] = y
    obf_ref[...] = y.astype(BF16)


def _res_ln_kernel(alpha, h_ref, y_ref, g_ref, b_ref, o_ref, obf_ref):
    y = _ln_rows(alpha * h_ref[...] + y_ref[...], g_ref[...], b_ref[...])
    o_ref[...] = y
    obf_ref[...] = y.astype(BF16)


def _layer_norm(x, g, b, residual=None, alpha=1.0, tr=256):
    t, d = x.shape
    row = pl.BlockSpec((tr, d), lambda i: (i, 0))
    vec = pl.BlockSpec((1, d), lambda i: (0, 0))
    outs = (jax.ShapeDtypeStruct((t, d), F32), jax.ShapeDtypeStruct((t, d), BF16))
    blk = _nbytes((tr, d), F32)
    if residual is None:
        kern, ins, specs, nblk = _ln_kernel, (x,), [row], 3
    else:
        kern = functools.partial(_res_ln_kernel, alpha)
        ins, specs, nblk = (residual, x), [row, row], 4
    return pl.pallas_call(
        kern, grid=(t // tr,), in_specs=specs + [vec, vec], out_specs=(row, row), out_shape=outs,
        name="layer_norm" if residual is None else "residual_layer_norm",
        compiler_params=pltpu.CompilerParams(
            dimension_semantics=("parallel",), vmem_limit_bytes=_vmem_limit([blk] * nblk)),
    )(*ins, g.reshape(1, d), b.reshape(1, d))


def _mm_wcast_kernel(a_ref, w_ref, o_ref, wb_ref):
    @pl.when(pl.program_id(1) == 0)
    def _():
        wb_ref[...] = w_ref[...].astype(BF16)
    o_ref[...] = jnp.dot(a_ref[...], wb_ref[...], preferred_element_type=F32).astype(o_ref.dtype)


def _matmul_wcast(a, w, layer, out_dtype, name, col_shift=0, tm=1024, tn=512):
    m, k = a.shape
    _, _, n = w.shape
    nb = n // tn
    blocks = [_nbytes((tm, k), a.dtype), _nbytes((k, tn), F32), _nbytes((tm, tn), out_dtype)]
    return pl.pallas_call(
        _mm_wcast_kernel, grid=(nb, m // tm),
        in_specs=[pl.BlockSpec((tm, k), lambda j, i: (i, 0)),
                  pl.BlockSpec((None, k, tn), lambda j, i: (layer, 0, (j + col_shift) % nb))],
        out_specs=pl.BlockSpec((tm, tn), lambda j, i: (i, j)),
        out_shape=jax.ShapeDtypeStruct((m, n), out_dtype), name=name,
        scratch_shapes=[pltpu.VMEM((k, tn), BF16)],
        compiler_params=pltpu.CompilerParams(
            dimension_semantics=("parallel", "arbitrary"),
            vmem_limit_bytes=_vmem_limit(blocks, _nbytes((k, tn), BF16) + _nbytes((tm, tn), F32))),
    )(a, w)


def _cast_pad_kernel(axis, period, nvalid, x_ref, o_ref):
    valid = pl.program_id(axis) % period < nvalid
    o_ref[...] = jnp.where(valid, x_ref[...], 0.0).astype(o_ref.dtype)


def _cast_pad(w, layer, axis, groups, blk, name):
    _, r, c = w.shape
    size = (r, c)[axis] // groups
    nvalid = size // blk
    period = -(-nvalid // 4) * 4
    last = groups * nvalid - 1

    def src(p):
        return jnp.minimum(p // period * nvalid + p % period, last)

    if axis == 0:
        grid, shape = (groups * period, 1), (groups * period * blk, c)
        ispec = pl.BlockSpec((None, blk, c), lambda p, q: (layer, src(p), 0))
        ospec = pl.BlockSpec((blk, c), lambda p, q: (p, 0))
        tile = (blk, c)
    else:
        grid, shape = (1, groups * period), (r, groups * period * blk)
        ispec = pl.BlockSpec((None, r, blk), lambda q, p: (layer, 0, src(p)))
        ospec = pl.BlockSpec((r, blk), lambda q, p: (0, p))
        tile = (r, blk)
    return pl.pallas_call(
        functools.partial(_cast_pad_kernel, axis, period, nvalid), grid=grid,
        in_specs=[ispec], out_specs=ospec, out_shape=jax.ShapeDtypeStruct(shape, BF16), name=name,
        compiler_params=pltpu.CompilerParams(
            dimension_semantics=("parallel", "parallel"),
            vmem_limit_bytes=_vmem_limit([_nbytes(tile, F32), _nbytes(tile, BF16)])),
    )(w)


def _mm_ksplit_kernel(a_ref, b_ref, o_ref):
    @pl.when(pl.program_id(2) == 0)
    def _():
        o_ref[...] = jnp.zeros_like(o_ref)
    o_ref[...] += jnp.dot(a_ref[...], b_ref[...], preferred_element_type=F32)


def _matmul_ksplit(a, b, name, tm=1024, tn=1024, tk=2816):
    m, k = a.shape
    _, n = b.shape
    blocks = [_nbytes((tm, tk), a.dtype), _nbytes((tk, tn), b.dtype), _nbytes((tm, tn), F32)]
    return pl.pallas_call(
        _mm_ksplit_kernel, grid=(m // tm, n // tn, k // tk),
        in_specs=[pl.BlockSpec((tm, tk), lambda i, j, kk: (i, kk)),
                  pl.BlockSpec((tk, tn), lambda i, j, kk: (kk, j))],
        out_specs=pl.BlockSpec((tm, tn), lambda i, j, kk: (i, j)),
        out_shape=jax.ShapeDtypeStruct((m, n), F32), name=name,
        compiler_params=pltpu.CompilerParams(
            dimension_semantics=("parallel", "parallel", "arbitrary"),
            vmem_limit_bytes=_vmem_limit(blocks, _nbytes((tm, tn), F32))),
    )(a, b)


def _halo_specs(seq_len, tr, cb, col_of):
    r = tr // HALO
    last = seq_len // HALO - 1

    def cur(b, i, *rest):
        return (b, i, col_of(*rest))

    def prev(b, i, *rest):
        return (b, jnp.maximum(i * r - 1, 0), col_of(*rest))

    def nxt(b, i, *rest):
        return (b, jnp.minimum((i + 1) * r, last), col_of(*rest))

    return [pl.BlockSpec((None, HALO, cb), prev), pl.BlockSpec((None, tr, cb), cur),
            pl.BlockSpec((None, HALO, cb), nxt)]


def _fill_ext(ext_ref, prev, cur, nxt):
    i = pl.program_id(1)
    tr = cur.shape[0]
    ext_ref[0:HALO, :] = jnp.where(i > 0, prev, 0.0)
    ext_ref[HALO:HALO + tr, :] = cur
    ext_ref[HALO + tr:2 * HALO + tr, :] = jnp.where(i < pl.num_programs(1) - 1, nxt, 0.0)


def _dwconv3_tile(ext_ref, w_ref, r0, rc, c0, cc):
    n = rc + 2 * SUBLANES
    lo = HALO + r0 - SUBLANES
    x = ext_ref[lo:lo + n, c0:c0 + cc]
    w = w_ref[:, c0:c0 + cc]
    y = w[0:1] * pltpu.roll(x, 1, 0) + w[1:2] * x + w[2:3] * pltpu.roll(x, n - 1, 0)
    return y[SUBLANES:SUBLANES + rc]


def _conf_kernel(vp, vc, vn, gp, gc, gn, w_ref, bias_ref, lg_ref, lb_ref, o_ref, ext_ref, sh_ref, conv_ref):
    def glu(v, g):
        return v[...] * jax.nn.sigmoid(g[...])

    _fill_ext(ext_ref, glu(vp, gp), glu(vc, gc), glu(vn, gn))
    tr, c = conv_ref.shape
    n_sh = ext_ref.shape[0] - SUBLANES
    for s in range(SUBLANES):
        sh_ref[s] = ext_ref[s:s + n_sh, :]
    first = HALO - CONF_WIDTH // 2
    for c0 in range(0, c, LANES):
        for r0 in range(0, tr, 64):
            acc = None
            for k in range(CONF_WIDTH):
                a, s = divmod(first + k, SUBLANES)
                term = w_ref[k:k + 1, c0:c0 + LANES] * sh_ref[s, SUBLANES * a + r0:SUBLANES * a + r0 + 64, c0:c0 + LANES]
                acc = term if acc is None else acc + term
            conv_ref[r0:r0 + 64, c0:c0 + LANES] = acc
    for r0 in range(0, tr, 16):
        y = _ln_rows(conv_ref[r0:r0 + 16, :] + bias_ref[...], lg_ref[...], lb_ref[...])
        o_ref[r0:r0 + 16, :] = (y * jax.nn.sigmoid(y)).astype(o_ref.dtype)


def _conformer_branch(u, w, bias, ln_g, ln_b, col0, dc, tr=128):
    b, l, _ = u.shape
    specs = _halo_specs(l, tr, dc, lambda: col0) + _halo_specs(l, tr, dc, lambda: col0 + 1)
    vec = pl.BlockSpec((1, dc), lambda bb, i: (0, 0))
    blocks = [_nbytes((tr + 2 * HALO, dc), F32)] * 2 + [_nbytes((CONF_WIDTH, dc), F32), _nbytes((tr, dc), BF16)]
    n_sh = tr + 2 * HALO - SUBLANES
    scratch = _nbytes((tr + 2 * HALO, dc), F32) + _nbytes((tr, dc), F32) + _nbytes((SUBLANES, n_sh, dc), F32)
    return pl.pallas_call(
        _conf_kernel, grid=(b, l // tr),
        in_specs=specs + [pl.BlockSpec((CONF_WIDTH, dc), lambda bb, i: (0, 0)), vec, vec, vec],
        out_specs=pl.BlockSpec((None, tr, dc), lambda bb, i: (bb, i, 0)),
        out_shape=jax.ShapeDtypeStruct((b, l, dc), BF16), name="conformer_branch",
        scratch_shapes=[pltpu.VMEM((tr + 2 * HALO, dc), F32), pltpu.VMEM((SUBLANES, n_sh, dc), F32),
                        pltpu.VMEM((tr, dc), F32)],
        compiler_params=pltpu.CompilerParams(
            dimension_semantics=("parallel", "parallel"), vmem_limit_bytes=_vmem_limit(blocks, scratch)),
    )(u, u, u, u, u, u, w, bias.reshape(1, dc), ln_g.reshape(1, dc), ln_b.reshape(1, dc))


def _sconv_kernel(gb_ref, cp, cc, cn, xp, xc, xn, w_ref, o_ref, ext_ref):
    _fill_ext(ext_ref, cp[...] * xp[...], cc[...] * xc[...], cn[...] * xn[...])
    tr, c = o_ref.shape
    for c0 in range(0, c, 256):
        for r0 in range(0, tr, CONV_ROWS):
            y = _dwconv3_tile(ext_ref, w_ref, r0, CONV_ROWS, c0, 256)
            o_ref[r0:r0 + CONV_ROWS, c0:c0 + 256] = (gb_ref[r0:r0 + CONV_ROWS, c0:c0 + 256] * y).astype(o_ref.dtype)


def _sconv_branch(u, w, col0, ds, tr=256):
    b, l, _ = u.shape
    specs = ([pl.BlockSpec((None, tr, ds), lambda bb, i: (bb, i, col0))]
             + _halo_specs(l, tr, ds, lambda: col0 + 1) + _halo_specs(l, tr, ds, lambda: col0 + 2))
    blocks = [_nbytes((tr + 2 * HALO, ds), F32)] * 3 + [_nbytes((tr, ds), BF16)]
    return pl.pallas_call(
        _sconv_kernel, grid=(b, l // tr),
        in_specs=specs + [pl.BlockSpec((SHORT_WIDTH, ds), lambda bb, i: (0, 0))],
        out_specs=pl.BlockSpec((None, tr, ds), lambda bb, i: (bb, i, 0)),
        out_shape=jax.ShapeDtypeStruct((b, l, ds), BF16), name="sconv_branch",
        scratch_shapes=[pltpu.VMEM((tr + 2 * HALO, ds), F32)],
        compiler_params=pltpu.CompilerParams(
            dimension_semantics=("parallel", "parallel"),
            vmem_limit_bytes=_vmem_limit(blocks, _nbytes((tr + 2 * HALO, ds), F32))),
    )(u, u, u, u, u, u, u, w)


def _hyena_short_kernel(xp, xc, xn, w_ref, v_ref, x1_ref, x2_ref, ext_ref):
    _fill_ext(ext_ref, xp[...], xc[...], xn[...])
    tr, dh = v_ref.shape
    for part, o_ref in enumerate((v_ref, x1_ref, x2_ref)):
        for c0 in range(0, dh, 256):
            for r0 in range(0, tr, CONV_ROWS):
                o_ref[r0:r0 + CONV_ROWS, c0:c0 + 256] = _dwconv3_tile(
                    ext_ref, w_ref, r0, CONV_ROWS, part * dh + c0, 256)


def _hyena_short_conv(u, w, col0, dh, tr=128):
    b, l, _ = u.shape
    out = jax.ShapeDtypeStruct((b, l, dh), F32)
    ospec = pl.BlockSpec((None, tr, dh), lambda bb, i: (bb, i, 0))
    blocks = [_nbytes((tr + 2 * HALO, 3 * dh), F32)] + [_nbytes((tr, dh), F32)] * 3
    return pl.pallas_call(
        _hyena_short_kernel, grid=(b, l // tr),
        in_specs=(_halo_specs(l, tr, 3 * dh, lambda: col0)
                  + [pl.BlockSpec((SHORT_WIDTH, 3 * dh), lambda bb, i: (0, 0))]),
        out_specs=(ospec, ospec, ospec), out_shape=(out, out, out), name="hyena_short_conv",
        scratch_shapes=[pltpu.VMEM((tr + 2 * HALO, 3 * dh), F32)],
        compiler_params=pltpu.CompilerParams(
            dimension_semantics=("parallel", "parallel"),
            vmem_limit_bytes=_vmem_limit(blocks, _nbytes((tr + 2 * HALO, 3 * dh), F32))),
    )(u, u, u, w)


MXU_COLS = 256
GATE_ROWS = 256


def _ffn_up_gate_kernel(tiles_per_seq, ap_ref, ac_ref, an_ref, wg_ref, wv_ref, cg_ref, cv_ref, o_ref, a_ext,
                        up_ref):
    tm, tn = o_ref.shape
    rows = tm + 2 * HALO

    @pl.when(pl.program_id(1) == 0)
    def _():
        t = pl.program_id(0) % tiles_per_seq
        a_ext[0:HALO, :] = jnp.where(t > 0, ap_ref[...], jnp.zeros_like(ap_ref))
        a_ext[HALO:HALO + tm, :] = ac_ref[...]
        a_ext[HALO + tm:, :] = jnp.where(t < tiles_per_seq - 1, an_ref[...], jnp.zeros_like(an_ref))

    def conv(x, c_ref, c0):
        n = x.shape[0]
        w = c_ref[:, c0:c0 + MXU_COLS]
        y = w[0:1] * pltpu.roll(x, 1, 0) + w[1:2] * x + w[2:3] * pltpu.roll(x, n - 1, 0)
        return y[SUBLANES:n - SUBLANES]

    for s, c0 in enumerate(range(0, tn, MXU_COLS)):
        a = a_ext[...]
        up_ref[0, s] = jnp.dot(a, wg_ref[:, c0:c0 + MXU_COLS], preferred_element_type=F32)
        up_ref[1, s] = jnp.dot(a, wv_ref[:, c0:c0 + MXU_COLS], preferred_element_type=F32)
        for r0 in range(0, tm, GATE_ROWS):
            lo, hi = HALO + r0 - SUBLANES, HALO + r0 + GATE_ROWS + SUBLANES
            g = conv(up_ref[0, s, lo:hi, :], cg_ref, c0)
            v = conv(up_ref[1, s, lo:hi, :], cv_ref, c0)
            hg = 0.5 * g
            o_ref[r0:r0 + GATE_ROWS, c0:c0 + MXU_COLS] = ((hg + hg * jnp.tanh(hg)) * v).astype(o_ref.dtype)


def _ffn_up_gate(h, wu, wc, seq_len, tm=1024, tn=512):
    t, d = h.shape
    f = wu.shape[1] // 2
    nc = f // tn
    r = tm // HALO
    last = t // HALO - 1
    specs = [pl.BlockSpec((HALO, d), lambda i, j: (jnp.maximum(i * r - 1, 0), 0)),
             pl.BlockSpec((tm, d), lambda i, j: (i, 0)),
             pl.BlockSpec((HALO, d), lambda i, j: (jnp.minimum((i + 1) * r, last), 0)),
             pl.BlockSpec((d, tn), lambda i, j: (0, j)), pl.BlockSpec((d, tn), lambda i, j: (0, j + nc)),
             pl.BlockSpec((SHORT_WIDTH, tn), lambda i, j: (0, j)),
             pl.BlockSpec((SHORT_WIDTH, tn), lambda i, j: (0, j + nc))]
    blocks = [_nbytes((tm + 2 * HALO, d), BF16), 2 * _nbytes((d, tn), BF16), _nbytes((tm, tn), BF16)]
    scratch = _nbytes((tm + 2 * HALO, d), BF16) + 4 * _nbytes((tm + 2 * HALO, MXU_COLS), F32)
    return pl.pallas_call(
        functools.partial(_ffn_up_gate_kernel, seq_len // tm), grid=(t // tm, nc),
        in_specs=specs, out_specs=pl.BlockSpec((tm, tn), lambda i, j: (i, j)),
        out_shape=jax.ShapeDtypeStruct((t, f), BF16), name="ffn_up_gate",
        scratch_shapes=[pltpu.VMEM((tm + 2 * HALO, d), BF16),
                        pltpu.VMEM((2, tn // MXU_COLS, tm + 2 * HALO, MXU_COLS), F32)],
        compiler_params=pltpu.CompilerParams(
            dimension_semantics=("parallel", "arbitrary"), vmem_limit_bytes=_vmem_limit(blocks, scratch)),
    )(h, h, h, wu, wu, wc, wc)


def _dft_constants(n1):
    n = n1 * n1
    h = n1 // 2
    idx = jnp.arange(n1, dtype=jnp.int32)
    ang1 = (2.0 * math.pi / n1) * ((idx[:, None] * idx[None, :]) % n1).astype(F32)
    c1, s1 = jnp.cos(ang1), jnp.sin(ang1)
    m1 = jnp.stack([jnp.concatenate([c1[:, :h], s1[:, :h]], axis=1),
                    jnp.concatenate([-s1[:, :h], c1[:, :h]], axis=1)], axis=1).reshape(2 * n1, n1)
    m1f = jnp.stack([c1, -s1], axis=1).reshape(2 * n1, n1)
    ct, st = c1[:h, :], s1[:h, :]
    m3 = jnp.concatenate([jnp.stack([ct, -st], axis=2).reshape(h, 2 * n1),
                          jnp.stack([st, ct], axis=2).reshape(h, 2 * n1)], axis=0) / n
    k = idx[:, None, None] + n1 * idx[None, :, None]
    ang2 = (2.0 * math.pi / n) * ((k * idx[None, None, :]) % n).astype(F32)
    c2, s2 = jnp.cos(ang2), jnp.sin(ang2)
    g = jnp.concatenate([jnp.concatenate([c2, s2], axis=2), jnp.concatenate([-s2, c2], axis=2)], axis=1)
    gt = jnp.swapaxes(g, 1, 2)
    return tuple(m.astype(BF16) for m in (m1, m1f, g, gt, m3))


def _gather_rows(ref, count):
    flat = ref.reshape(count * SUBLANES, LANES)
    return jnp.concatenate([flat[pl.ds(j, count, stride=SUBLANES), :] for j in range(SUBLANES)], axis=1)


def _scatter_rows(ref, val):
    count = val.shape[0]
    flat = ref.reshape(count * SUBLANES, LANES)
    for j in range(SUBLANES):
        flat[pl.ds(j, count, stride=SUBLANES), :] = val[:, j * LANES:(j + 1) * LANES]


def _slab_specs(rows):
    return [pl.BlockSpec((rows, SUBLANES, LANES), lambda ct, jb, q=q: (0, jb, ct * DFT_SLABS + q))
            for q in range(DFT_SLABS)]


def _gather_slabs(refs):
    return jnp.concatenate([_gather_rows(r, r.shape[0]) for r in refs], axis=1).astype(BF16)


def _stage1_kernel(m_ref, *refs):
    x_refs, o_ref, slab_refs = refs[:DFT_SLABS], refs[DFT_SLABS], refs[DFT_SLABS + 1:]
    res = jnp.dot(m_ref[...], _gather_slabs(x_refs), preferred_element_type=F32)
    cols = SUBLANES * LANES
    for q in range(DFT_SLABS):
        _scatter_rows(slab_refs[q], res[:, q * cols:(q + 1) * cols])
        o_ref[:, :, q * LANES:(q + 1) * LANES] = slab_refs[q][...]


def _dft_stage1(m, x, name):
    r, k = m.shape
    _, n2, c = x.shape
    wide = DFT_SLABS * LANES
    blocks = [_nbytes((k, SUBLANES, wide), F32), _nbytes((r, SUBLANES, wide), F32)]
    return pl.pallas_call(
        _stage1_kernel, grid=(c // wide, n2 // SUBLANES),
        in_specs=[pl.BlockSpec((r, k), lambda ct, jb: (0, 0))] + _slab_specs(k),
        out_specs=pl.BlockSpec((r, SUBLANES, wide), lambda ct, jb: (0, jb, ct)),
        out_shape=jax.ShapeDtypeStruct((r, n2, c), F32), name=name,
        scratch_shapes=[pltpu.VMEM((r, SUBLANES, LANES), F32)] * DFT_SLABS,
        compiler_params=pltpu.CompilerParams(
            dimension_semantics=("parallel", "parallel"), vmem_limit_bytes=_vmem_limit(blocks, 3 * blocks[1])),
    )(m, *([x] * DFT_SLABS))


def _stage2_kernel(g_ref, gt_ref, z_ref, t_ref, o_ref):
    two, n1, c = z_ref.shape
    g = g_ref[...]
    x = jnp.dot(g, z_ref[...].reshape(two * n1, c).astype(BF16), preferred_element_type=F32)
    h = jnp.dot(g, t_ref[...].reshape(two * n1, c).astype(BF16), preferred_element_type=F32)
    xr, xi, hr, hi = x[:n1], x[n1:], h[:n1], h[n1:]
    y = jnp.concatenate([xr * hr - xi * hi, xr * hi + xi * hr], axis=0).astype(BF16)
    o_ref[...] = jnp.dot(gt_ref[...], y, preferred_element_type=F32).reshape(two, n1, c)


def _dft_stage2(g, gt, z, tf, order):
    n1, _, _, c = z.shape
    blocks = [_nbytes((2 * n1, 2 * n1), BF16)] * 2 + [_nbytes((2, n1, c), F32)] * 3
    return pl.pallas_call(
        _stage2_kernel, grid=(n1,),
        in_specs=[pl.BlockSpec((None, 2 * n1, 2 * n1), lambda k: (k, 0, 0)),
                  pl.BlockSpec((None, 2 * n1, 2 * n1), lambda k: (k, 0, 0)),
                  pl.BlockSpec((None, 2, n1, c), lambda k: (k, 0, 0, 0)),
                  pl.BlockSpec((None, 2, n1, c), lambda k: (k, 0, 0, order))],
        out_specs=pl.BlockSpec((None, 2, n1, c), lambda k: (k, 0, 0, 0)),
        out_shape=jax.ShapeDtypeStruct((n1, 2, n1, c), F32), name="dft_stage2",
        compiler_params=pltpu.CompilerParams(
            dimension_semantics=("parallel",), vmem_limit_bytes=_vmem_limit(blocks, 5 * blocks[3])),
    )(g, gt, z, tf)


def _stage3_kernel(m_ref, *refs):
    b_refs = refs[:DFT_SLABS]
    u_ref, x_ref, d_ref, o_ref = refs[DFT_SLABS:DFT_SLABS + 4]
    y_refs = refs[DFT_SLABS + 4:]
    y = jnp.dot(m_ref[...], _gather_slabs(b_refs), preferred_element_type=F32)
    cols = SUBLANES * LANES
    for q in range(DFT_SLABS):
        _scatter_rows(y_refs[q], y[:, q * cols:(q + 1) * cols])
        lanes = slice(q * LANES, (q + 1) * LANES)
        o_ref[:, :, lanes] = x_ref[:, :, lanes] * (y_refs[q][...] + u_ref[:, :, lanes] * d_ref[:, :, lanes])


def _dft_stage3(m3, bz, u, xg, d):
    r, k = m3.shape
    _, n2, c = u.shape
    wide = DFT_SLABS * LANES
    tile = pl.BlockSpec((r, SUBLANES, wide), lambda ct, jb: (0, jb, ct))
    blocks = [_nbytes((k, SUBLANES, wide), F32)] + [_nbytes((r, SUBLANES, wide), F32)] * 3
    return pl.pallas_call(
        _stage3_kernel, grid=(c // wide, n2 // SUBLANES),
        in_specs=([pl.BlockSpec((r, k), lambda ct, jb: (0, 0))] + _slab_specs(k)
                  + [tile, tile, pl.BlockSpec((1, 1, wide), lambda ct, jb: (0, 0, ct))]),
        out_specs=tile, out_shape=jax.ShapeDtypeStruct((r, n2, c), F32), name="dft_stage3",
        scratch_shapes=[pltpu.VMEM((r, SUBLANES, LANES), F32)] * DFT_SLABS,
        compiler_params=pltpu.CompilerParams(
            dimension_semantics=("parallel", "parallel"), vmem_limit_bytes=_vmem_limit(blocks, 3 * blocks[0])),
    )(m3, *([bz] * DFT_SLABS), u, xg, d.reshape(1, 1, c))


def _long_conv_gate(consts, u, xg, tf, order, d):
    m1, _, g, gt, m3 = consts
    b, l, c = u.shape
    n1 = m1.shape[1]
    u3 = u.reshape(n1, n1, c)
    z = _dft_stage1(m1, u3, "dft_stage1").reshape(n1, 2, n1, c)
    bz = _dft_stage2(g, gt, z, tf, order).reshape(2 * n1, n1, c)
    return _dft_stage3(m3, bz, u3, xg.reshape(n1, n1, c), d).reshape(b, l, c)


def _filter_mlp_kernel(seq_len, z_ref, t_ref, w1_ref, b1_ref, w2_ref, b2_ref, fr_ref, w3_ref, dec_ref, o_ref):
    hp = lax.Precision.HIGHEST
    fr = fr_ref[...]
    h = jnp.sin(fr * (jnp.dot(z_ref[...], w1_ref[...], precision=hp, preferred_element_type=F32) + b1_ref[...]))
    h = jnp.sin(fr * (jnp.dot(h, w2_ref[...], precision=hp, preferred_element_type=F32) + b2_ref[...]))
    h = jnp.dot(h.astype(BF16), w3_ref[...].astype(BF16), preferred_element_type=F32)
    h = h * jnp.exp(-t_ref[...] * dec_ref[...])
    tf = o_ref.shape[0]
    row = pl.program_id(0) * tf + lax.broadcasted_iota(jnp.int32, (tf, 1), 0)
    o_ref[...] = jnp.where(row == seq_len, 0.0, h).astype(o_ref.dtype)


def _hyena_taps(z_ext, t_ext, w1, b1, w2, b2, w3, freq, decay, tf=512):
    n2l, e = z_ext.shape
    seq_len = n2l // 2
    hid = w1.shape[1]
    order, _, c = decay.shape
    w3d = w3.reshape(hid, order, 2, c).transpose(2, 0, 1, 3).reshape(2, hid, order * c)
    dec = jnp.abs(decay).transpose(1, 0, 2).reshape(2, 1, order * c)
    half = seq_len // tf
    full = lambda shape: pl.BlockSpec(shape, lambda i: (0,) * len(shape))
    blocks = [_nbytes((tf, 128), F32) * 2, _nbytes((hid, order * c), F32), _nbytes((tf, order * c), F32)]
    return pl.pallas_call(
        functools.partial(_filter_mlp_kernel, seq_len), grid=(n2l // tf,),
        in_specs=[pl.BlockSpec((tf, e), lambda i: (i, 0)), pl.BlockSpec((tf, 1), lambda i: (i, 0)),
                  full((e, hid)), full((1, hid)), full((hid, hid)), full((1, hid)), full((1, hid)),
                  pl.BlockSpec((None, hid, order * c), lambda i: (i // half, 0, 0)),
                  pl.BlockSpec((None, 1, order * c), lambda i: (i // half, 0, 0))],
        out_specs=pl.BlockSpec((tf, order * c), lambda i: (i, 0)),
        out_shape=jax.ShapeDtypeStruct((n2l, order * c), F32), name="filter_taps",
        compiler_params=pltpu.CompilerParams(
            dimension_semantics=("parallel",),
            vmem_limit_bytes=_vmem_limit(blocks, 3 * _nbytes((tf, order * c), F32))),
    )(z_ext, t_ext, w1, b1.reshape(1, hid), w2, b2.reshape(1, hid), freq.reshape(1, hid), w3d, dec)


def _pos_features(seq_len, emb_dim):
    t = jnp.linspace(0.0, 1.0, seq_len, dtype=F32)[:, None]
    bands = (emb_dim - 1) // 2
    w = 2.0 * math.pi * jnp.arange(seq_len, dtype=F32)[:, None] / seq_len
    f = jnp.linspace(1e-4, bands - 1, bands, dtype=F32)[None, :]
    z = jnp.concatenate([t, jnp.cos(f * w), -jnp.sin(f * w)], axis=-1)
    pad = (-emb_dim) % 8
    z = jnp.pad(z, ((0, 0), (0, pad)))
    ext = lambda a: jnp.concatenate([a, a[:1], a[:0:-1]], axis=0)
    return ext(z), ext(t), pad


def kernel(x, ln_emb_g, ln_emb_b, w_in, conf_dw_w, conf_dw_b, conf_ln_g, conf_ln_b, sconv_w, hyena_short_w, hyena_w1, hyena_b1, hyena_w2, hyena_b2, hyena_w3, hyena_freq, hyena_decay, hyena_d, w_out, ln1_g, ln1_b, w_up, ffn_dw_w, w_down, ln2_g, ln2_b):
    b, l, d = x.shape
    depth = w_in.shape[0]
    dc = conf_dw_w.shape[2]
    ds = sconv_w.shape[2]
    dh = hyena_d.shape[2]
    dff = w_down.shape[1]
    dffp = -(-dff // 1024) * 1024
    alpha = (2.0 * depth) ** 0.25
    t = b * l
    n1 = math.isqrt(2 * l)
    assert b == 2 and n1 * n1 == 2 * l, "long conv packs two batch rows into one complex 2L = n1*n1 transform"

    consts = _dft_constants(n1)
    z_ext, t_ext, epad = _pos_features(l, hyena_w1.shape[1])

    h32, h16 = _layer_norm(x.reshape(t, d), ln_emb_g, ln_emb_b)
    for i in range(depth):
        taps = _hyena_taps(z_ext, t_ext, jnp.pad(hyena_w1[i], ((0, epad), (0, 0))), hyena_b1[i], hyena_w2[i],
                           hyena_b2[i], hyena_w3[i], hyena_freq[i], hyena_decay[i])
        oc = taps.shape[1]
        tf = _dft_stage1(consts[1], taps.reshape(n1, n1, oc), "filter_stage1").reshape(n1, 2, n1, oc)

        split = 2 * dc + 3 * ds
        tn = 512
        assert split % tn == 0
        u = _matmul_wcast(h16, w_in, i, F32, "mm_in", col_shift=split // tn, tn=tn).reshape(b, l, -1)
        a = _conformer_branch(u, conf_dw_w[i], conf_dw_b[i], conf_ln_g[i], conf_ln_b[i], 3 * dh // dc, dc)
        bb = _sconv_branch(u, sconv_w[i], (3 * dh + 2 * dc) // ds, ds)
        v, x1, x2 = _hyena_short_conv(u, hyena_short_w[i], 0, dh)
        zc = _long_conv_gate(consts, v, x1, tf, 0, hyena_d[i, 0])
        zc = _long_conv_gate(consts, zc, x2, tf, 1, hyena_d[i, 1])
        cat = jnp.concatenate([a, bb, zc.astype(BF16)], axis=-1).reshape(t, -1)
        mix = _matmul_wcast(cat, w_out, i, F32, "mm_out", tn=tn)
        h32, h16 = _layer_norm(mix, ln1_g[i], ln1_b[i], residual=h32, alpha=alpha)

        pad = dffp - dff
        wu = _cast_pad(w_up, i, 1, 2, MXU_COLS, "cast_w_up")
        wd = _cast_pad(w_down, i, 0, 1, MXU_COLS, "cast_w_down")
        wc = jnp.concatenate([jnp.pad(ffn_dw_w[i, :, :dff], ((0, 0), (0, pad))),
                              jnp.pad(ffn_dw_w[i, :, dff:], ((0, 0), (0, pad)))], axis=1)
        gated = _ffn_up_gate(h16, wu, wc, l)
        ffn = _matmul_ksplit(gated, wd, "mm_down", tk=dffp // 4)
        h32, h16 = _layer_norm(ffn, ln2_g[i], ln2_b[i], residual=h32, alpha=alpha)
    return h32.reshape(b, l, d)
```

```python
import functools
import math

import jax
import jax.numpy as jnp
from jax import lax
from jax.experimental import pallas as pl
from jax.experimental.pallas import tpu as pltpu

F32 = jnp.float32
BF16 = jnp.bfloat16

LN_EPS = 1e-5
CONF_WIDTH = 31
SHORT_WIDTH = 3
HALO = 16
LANES = 128
SUBLANES = 8
DFT_SLABS = 4
CONV_ROWS = 128
V7X_VMEM_BYTES = 64 * 1024 * 1024
VMEM_CAP = V7X_VMEM_BYTES - 8 * 1024 * 1024


def _vmem_limit(block_bytes, scratch_bytes=0):
    est = 2 * sum(block_bytes) + scratch_bytes
    return int(min(VMEM_CAP, max(32 * 1024 * 1024, 2 * est)))


def _nbytes(shape, dtype):
    return math.prod(shape) * jnp.dtype(dtype).itemsize


def _ln_rows(y, g, b):
    mu = jnp.mean(y, axis=-1, keepdims=True)
    d = y - mu
    var = jnp.mean(d * d, axis=-1, keepdims=True)
    return d * lax.rsqrt(var + LN_EPS) * g + b


def _ln_kernel(x_ref, g_ref, b_ref, o_ref, obf_ref):
    y = _ln_rows(x_ref[...], g_ref[...], b_ref[...])
    o_ref[...] = y
    obf_ref[...] = y.astype(BF16)


def _res_ln_kernel(alpha, h_ref, y_ref, g_ref, b_ref, o_ref, obf_ref):
    y = _ln_rows(alpha * h_ref[...] + y_ref[...], g_ref[...], b_ref[...])
    o_ref[...] = y
    obf_ref[...] = y.astype(BF16)


def _layer_norm(x, g, b, residual=None, alpha=1.0, tr=256):
    t, d = x.shape
    row = pl.BlockSpec((tr, d), lambda i: (i, 0))
    vec = pl.BlockSpec((1, d), lambda i: (0, 0))
    outs = (jax.ShapeDtypeStruct((t, d), F32), jax.ShapeDtypeStruct((t, d), BF16))
    blk = _nbytes((tr, d), F32)
    if residual is None:
        kern, ins, specs, nblk = _ln_kernel, (x,), [row], 3
    else:
        kern = functools.partial(_res_ln_kernel, alpha)
        ins, specs, nblk = (residual, x), [row, row], 4
    return pl.pallas_call(
        kern, grid=(t // tr,), in_specs=specs + [vec, vec], out_specs=(row, row), out_shape=outs,
        name="layer_norm" if residual is None else "residual_layer_norm",
        compiler_params=pltpu.CompilerParams(
            dimension_semantics=("parallel",), vmem_limit_bytes=_vmem_limit([blk] * nblk)),
    )(*ins, g.reshape(1, d), b.reshape(1, d))


def _mm_kernel(a_ref, b_ref, o_ref):
    o_ref[...] = jnp.dot(a_ref[...], b_ref[...], preferred_element_type=F32).astype(o_ref.dtype)


def _matmul(a, b, out_dtype, name, tm=1024, tn=1024):
    m, k = a.shape
    _, n = b.shape
    blocks = [_nbytes((tm, k), a.dtype), _nbytes((k, tn), b.dtype), _nbytes((tm, tn), out_dtype)]
    return pl.pallas_call(
        _mm_kernel, grid=(m // tm, n // tn),
        in_specs=[pl.BlockSpec((tm, k), lambda i, j: (i, 0)), pl.BlockSpec((k, tn), lambda i, j: (0, j))],
        out_specs=pl.BlockSpec((tm, tn), lambda i, j: (i, j)),
        out_shape=jax.ShapeDtypeStruct((m, n), out_dtype), name=name,
        compiler_params=pltpu.CompilerParams(
            dimension_semantics=("parallel", "parallel"),
            vmem_limit_bytes=_vmem_limit(blocks, _nbytes((tm, tn), F32))),
    )(a, b)


def _cast_pad_kernel(axis, period, nvalid, x_ref, o_ref):
    valid = pl.program_id(axis) % period < nvalid
    o_ref[...] = jnp.where(valid, x_ref[...], 0.0).astype(o_ref.dtype)


def _cast_pad(w, layer, axis, groups, blk, name, rotate=0):
    _, r, c = w.shape
    size = (r, c)[axis] // groups
    nvalid = size // blk
    period = -(-nvalid // 4) * 4
    last = groups * nvalid - 1
    assert rotate == 0 or period == nvalid

    def src(p):
        if rotate:
            return (p + rotate) % (groups * nvalid)
        return jnp.minimum(p // period * nvalid + p % period, last)

    if axis == 0:
        grid, shape = (groups * period, 1), (groups * period * blk, c)
        ispec = pl.BlockSpec((None, blk, c), lambda p, q: (layer, src(p), 0))
        ospec = pl.BlockSpec((blk, c), lambda p, q: (p, 0))
        tile = (blk, c)
    else:
        grid, shape = (1, groups * period), (r, groups * period * blk)
        ispec = pl.BlockSpec((None, r, blk), lambda q, p: (layer, 0, src(p)))
        ospec = pl.BlockSpec((r, blk), lambda q, p: (0, p))
        tile = (r, blk)
    return pl.pallas_call(
        functools.partial(_cast_pad_kernel, axis, period, nvalid), grid=grid,
        in_specs=[ispec], out_specs=ospec, out_shape=jax.ShapeDtypeStruct(shape, BF16), name=name,
        compiler_params=pltpu.CompilerParams(
            dimension_semantics=("parallel", "parallel"),
            vmem_limit_bytes=_vmem_limit([_nbytes(tile, F32), _nbytes(tile, BF16)])),
    )(w)


def _mm_ksplit_kernel(a_ref, b_ref, o_ref):
    @pl.when(pl.program_id(2) == 0)
    def _():
        o_ref[...] = jnp.zeros_like(o_ref)
    o_ref[...] += jnp.dot(a_ref[...], b_ref[...], preferred_element_type=F32)


def _matmul_ksplit(a, b, name, tm=1024, tn=1024, tk=2816):
    m, k = a.shape
    _, n = b.shape
    blocks = [_nbytes((tm, tk), a.dtype), _nbytes((tk, tn), b.dtype), _nbytes((tm, tn), F32)]
    return pl.pallas_call(
        _mm_ksplit_kernel, grid=(m // tm, n // tn, k // tk),
        in_specs=[pl.BlockSpec((tm, tk), lambda i, j, kk: (i, kk)),
                  pl.BlockSpec((tk, tn), lambda i, j, kk: (kk, j))],
        out_specs=pl.BlockSpec((tm, tn), lambda i, j, kk: (i, j)),
        out_shape=jax.ShapeDtypeStruct((m, n), F32), name=name,
        compiler_params=pltpu.CompilerParams(
            dimension_semantics=("parallel", "parallel", "arbitrary"),
            vmem_limit_bytes=_vmem_limit(blocks, _nbytes((tm, tn), F32))),
    )(a, b)


def _halo_specs(seq_len, tr, cb, col_of):
    r = tr // HALO
    last = seq_len // HALO - 1

    def cur(b, i, *rest):
        return (b, i, col_of(*rest))

    def prev(b, i, *rest):
        return (b, jnp.maximum(i * r - 1, 0), col_of(*rest))

    def nxt(b, i, *rest):
        return (b, jnp.minimum((i + 1) * r, last), col_of(*rest))

    return [pl.BlockSpec((None, HALO, cb), prev), pl.BlockSpec((None, tr, cb), cur),
            pl.BlockSpec((None, HALO, cb), nxt)]


def _fill_ext(ext_ref, prev, cur, nxt):
    i = pl.program_id(1)
    tr = cur.shape[0]
    ext_ref[0:HALO, :] = jnp.where(i > 0, prev, 0.0)
    ext_ref[HALO:HALO + tr, :] = cur
    ext_ref[HALO + tr:2 * HALO + tr, :] = jnp.where(i < pl.num_programs(1) - 1, nxt, 0.0)


def _dwconv3_tile(ext_ref, w_ref, r0, rc, c0, cc):
    n = rc + 2 * SUBLANES
    lo = HALO + r0 - SUBLANES
    x = ext_ref[lo:lo + n, c0:c0 + cc]
    w = w_ref[:, c0:c0 + cc]
    y = w[0:1] * pltpu.roll(x, 1, 0) + w[1:2] * x + w[2:3] * pltpu.roll(x, n - 1, 0)
    return y[SUBLANES:SUBLANES + rc]


def _conf_kernel(vp, vc, vn, gp, gc, gn, w_ref, bias_ref, lg_ref, lb_ref, o_ref, ext_ref, sh_ref, conv_ref):
    def glu(v, g):
        return v[...].astype(F32) * jax.nn.sigmoid(g[...].astype(F32))

    _fill_ext(ext_ref, glu(vp, gp), glu(vc, gc), glu(vn, gn))
    tr, c = conv_ref.shape
    n_sh = ext_ref.shape[0] - SUBLANES
    for s in range(SUBLANES):
        sh_ref[s] = ext_ref[s:s + n_sh, :]
    first = HALO - CONF_WIDTH // 2
    for c0 in range(0, c, LANES):
        for r0 in range(0, tr, 64):
            acc = None
            for k in range(CONF_WIDTH):
                a, s = divmod(first + k, SUBLANES)
                term = w_ref[k:k + 1, c0:c0 + LANES] * sh_ref[s, SUBLANES * a + r0:SUBLANES * a + r0 + 64, c0:c0 + LANES]
                acc = term if acc is None else acc + term
            conv_ref[r0:r0 + 64, c0:c0 + LANES] = acc
    for r0 in range(0, tr, 16):
        y = _ln_rows(conv_ref[r0:r0 + 16, :] + bias_ref[...], lg_ref[...], lb_ref[...])
        o_ref[r0:r0 + 16, :] = (y * jax.nn.sigmoid(y)).astype(o_ref.dtype)


def _conformer_branch(u, w, bias, ln_g, ln_b, col0, dc, tr=128):
    b, l, _ = u.shape
    specs = _halo_specs(l, tr, dc, lambda: col0) + _halo_specs(l, tr, dc, lambda: col0 + 1)
    vec = pl.BlockSpec((1, dc), lambda bb, i: (0, 0))
    blocks = [_nbytes((tr + 2 * HALO, dc), F32)] * 2 + [_nbytes((CONF_WIDTH, dc), F32), _nbytes((tr, dc), BF16)]
    n_sh = tr + 2 * HALO - SUBLANES
    scratch = _nbytes((tr + 2 * HALO, dc), F32) + _nbytes((tr, dc), F32) + _nbytes((SUBLANES, n_sh, dc), F32)
    return pl.pallas_call(
        _conf_kernel, grid=(b, l // tr),
        in_specs=specs + [pl.BlockSpec((CONF_WIDTH, dc), lambda bb, i: (0, 0)), vec, vec, vec],
        out_specs=pl.BlockSpec((None, tr, dc), lambda bb, i: (bb, i, 0)),
        out_shape=jax.ShapeDtypeStruct((b, l, dc), BF16), name="conformer_branch",
        scratch_shapes=[pltpu.VMEM((tr + 2 * HALO, dc), F32), pltpu.VMEM((SUBLANES, n_sh, dc), F32),
                        pltpu.VMEM((tr, dc), F32)],
        compiler_params=pltpu.CompilerParams(
            dimension_semantics=("parallel", "parallel"), vmem_limit_bytes=_vmem_limit(blocks, scratch)),
    )(u, u, u, u, u, u, w, bias.reshape(1, dc), ln_g.reshape(1, dc), ln_b.reshape(1, dc))


def _sconv_kernel(gb_ref, cp, cc, cn, xp, xc, xn, w_ref, o_ref, ext_ref):
    prod = lambda a, b: a[...].astype(F32) * b[...].astype(F32)
    _fill_ext(ext_ref, prod(cp, xp), prod(cc, xc), prod(cn, xn))
    tr, c = o_ref.shape
    for c0 in range(0, c, 256):
        for r0 in range(0, tr, CONV_ROWS):
            y = _dwconv3_tile(ext_ref, w_ref, r0, CONV_ROWS, c0, 256)
            o_ref[r0:r0 + CONV_ROWS, c0:c0 + 256] = (gb_ref[r0:r0 + CONV_ROWS, c0:c0 + 256].astype(F32) * y).astype(o_ref.dtype)


def _sconv_branch(u, w, col0, ds, tr=256):
    b, l, _ = u.shape
    specs = ([pl.BlockSpec((None, tr, ds), lambda bb, i: (bb, i, col0))]
             + _halo_specs(l, tr, ds, lambda: col0 + 1) + _halo_specs(l, tr, ds, lambda: col0 + 2))
    blocks = [_nbytes((tr + 2 * HALO, ds), F32)] * 3 + [_nbytes((tr, ds), BF16)]
    return pl.pallas_call(
        _sconv_kernel, grid=(b, l // tr),
        in_specs=specs + [pl.BlockSpec((SHORT_WIDTH, ds), lambda bb, i: (0, 0))],
        out_specs=pl.BlockSpec((None, tr, ds), lambda bb, i: (bb, i, 0)),
        out_shape=jax.ShapeDtypeStruct((b, l, ds), BF16), name="sconv_branch",
        scratch_shapes=[pltpu.VMEM((tr + 2 * HALO, ds), F32)],
        compiler_params=pltpu.CompilerParams(
            dimension_semantics=("parallel", "parallel"),
            vmem_limit_bytes=_vmem_limit(blocks, _nbytes((tr + 2 * HALO, ds), F32))),
    )(u, u, u, u, u, u, u, w)


def _hyena_short_kernel(xp, xc, xn, w_ref, v_ref, x1_ref, x2_ref, ext_ref):
    _fill_ext(ext_ref, xp[...].astype(F32), xc[...].astype(F32), xn[...].astype(F32))
    tr, dh = v_ref.shape
    for part, o_ref in enumerate((v_ref, x1_ref, x2_ref)):
        for c0 in range(0, dh, 256):
            for r0 in range(0, tr, CONV_ROWS):
                o_ref[r0:r0 + CONV_ROWS, c0:c0 + 256] = _dwconv3_tile(
                    ext_ref, w_ref, r0, CONV_ROWS, part * dh + c0, 256)


def _hyena_short_conv(u, w, col0, dh, tr=128):
    b, l, _ = u.shape
    out = jax.ShapeDtypeStruct((b, l, dh), F32)
    ospec = pl.BlockSpec((None, tr, dh), lambda bb, i: (bb, i, 0))
    blocks = [_nbytes((tr + 2 * HALO, 3 * dh), F32)] + [_nbytes((tr, dh), F32)] * 3
    return pl.pallas_call(
        _hyena_short_kernel, grid=(b, l // tr),
        in_specs=(_halo_specs(l, tr, 3 * dh, lambda: col0)
                  + [pl.BlockSpec((SHORT_WIDTH, 3 * dh), lambda bb, i: (0, 0))]),
        out_specs=(ospec, ospec, ospec), out_shape=(out, out, out), name="hyena_short_conv",
        scratch_shapes=[pltpu.VMEM((tr + 2 * HALO, 3 * dh), F32)],
        compiler_params=pltpu.CompilerParams(
            dimension_semantics=("parallel", "parallel"),
            vmem_limit_bytes=_vmem_limit(blocks, _nbytes((tr + 2 * HALO, 3 * dh), F32))),
    )(u, u, u, w)


MXU_COLS = 256
GATE_ROWS = 256


def _ffn_up_gate_kernel(tiles_per_seq, ap_ref, ac_ref, an_ref, wg_ref, wv_ref, cg_ref, cv_ref, o_ref, a_ext,
                        up_ref):
    tm, tn = o_ref.shape
    rows = tm + 2 * HALO

    @pl.when(pl.program_id(1) == 0)
    def _():
        t = pl.program_id(0) % tiles_per_seq
        a_ext[0:HALO, :] = jnp.where(t > 0, ap_ref[...], jnp.zeros_like(ap_ref))
        a_ext[HALO:HALO + tm, :] = ac_ref[...]
        a_ext[HALO + tm:, :] = jnp.where(t < tiles_per_seq - 1, an_ref[...], jnp.zeros_like(an_ref))

    def conv(x, c_ref, c0):
        n = x.shape[0]
        w = c_ref[:, c0:c0 + MXU_COLS]
        y = w[0:1] * pltpu.roll(x, 1, 0) + w[1:2] * x + w[2:3] * pltpu.roll(x, n - 1, 0)
        return y[SUBLANES:n - SUBLANES]

    for s, c0 in enumerate(range(0, tn, MXU_COLS)):
        a = a_ext[...]
        up_ref[0, s] = jnp.dot(a, wg_ref[:, c0:c0 + MXU_COLS], preferred_element_type=F32)
        up_ref[1, s] = jnp.dot(a, wv_ref[:, c0:c0 + MXU_COLS], preferred_element_type=F32)
        for r0 in range(0, tm, GATE_ROWS):
            lo, hi = HALO + r0 - SUBLANES, HALO + r0 + GATE_ROWS + SUBLANES
            g = conv(up_ref[0, s, lo:hi, :], cg_ref, c0)
            v = conv(up_ref[1, s, lo:hi, :], cv_ref, c0)
            hg = 0.5 * g
            o_ref[r0:r0 + GATE_ROWS, c0:c0 + MXU_COLS] = ((hg + hg * jnp.tanh(hg)) * v).astype(o_ref.dtype)


def _ffn_up_gate(h, wu, wc, seq_len, tm=1024, tn=512):
    t, d = h.shape
    f = wu.shape[1] // 2
    nc = f // tn
    r = tm // HALO
    last = t // HALO - 1
    specs = [pl.BlockSpec((HALO, d), lambda i, j: (jnp.maximum(i * r - 1, 0), 0)),
             pl.BlockSpec((tm, d), lambda i, j: (i, 0)),
             pl.BlockSpec((HALO, d), lambda i, j: (jnp.minimum((i + 1) * r, last), 0)),
             pl.BlockSpec((d, tn), lambda i, j: (0, j)), pl.BlockSpec((d, tn), lambda i, j: (0, j + nc)),
             pl.BlockSpec((SHORT_WIDTH, tn), lambda i, j: (0, j)),
             pl.BlockSpec((SHORT_WIDTH, tn), lambda i, j: (0, j + nc))]
    blocks = [_nbytes((tm + 2 * HALO, d), BF16), 2 * _nbytes((d, tn), BF16), _nbytes((tm, tn), BF16)]
    scratch = _nbytes((tm + 2 * HALO, d), BF16) + 4 * _nbytes((tm + 2 * HALO, MXU_COLS), F32)
    return pl.pallas_call(
        functools.partial(_ffn_up_gate_kernel, seq_len // tm), grid=(t // tm, nc),
        in_specs=specs, out_specs=pl.BlockSpec((tm, tn), lambda i, j: (i, j)),
        out_shape=jax.ShapeDtypeStruct((t, f), BF16), name="ffn_up_gate",
        scratch_shapes=[pltpu.VMEM((tm + 2 * HALO, d), BF16),
                        pltpu.VMEM((2, tn // MXU_COLS, tm + 2 * HALO, MXU_COLS), F32)],
        compiler_params=pltpu.CompilerParams(
            dimension_semantics=("parallel", "arbitrary"), vmem_limit_bytes=_vmem_limit(blocks, scratch)),
    )(h, h, h, wu, wu, wc, wc)


def _dft_constants(n1):
    n = n1 * n1
    h = n1 // 2
    idx = jnp.arange(n1, dtype=jnp.int32)
    ang1 = (2.0 * math.pi / n1) * ((idx[:, None] * idx[None, :]) % n1).astype(F32)
    c1, s1 = jnp.cos(ang1), jnp.sin(ang1)
    m1 = jnp.stack([jnp.concatenate([c1[:, :h], s1[:, :h]], axis=1),
                    jnp.concatenate([-s1[:, :h], c1[:, :h]], axis=1)], axis=1).reshape(2 * n1, n1)
    m1f = jnp.stack([c1, -s1], axis=1).reshape(2 * n1, n1)
    ct, st = c1[:h, :], s1[:h, :]
    m3 = jnp.concatenate([jnp.stack([ct, -st], axis=2).reshape(h, 2 * n1),
                          jnp.stack([st, ct], axis=2).reshape(h, 2 * n1)], axis=0) / n
    k = idx[:, None, None] + n1 * idx[None, :, None]
    ang2 = (2.0 * math.pi / n) * ((k * idx[None, None, :]) % n).astype(F32)
    c2, s2 = jnp.cos(ang2), jnp.sin(ang2)
    g = jnp.concatenate([jnp.concatenate([c2, s2], axis=2), jnp.concatenate([-s2, c2], axis=2)], axis=1)
    gt = jnp.swapaxes(g, 1, 2)
    g = g.reshape(n1, 2 * n1, 2, n1).swapaxes(2, 3).reshape(n1, 2 * n1, 2 * n1)
    gt = gt.reshape(n1, 2, n1, 2 * n1).swapaxes(1, 2).reshape(n1, 2 * n1, 2 * n1)
    return tuple(m.astype(BF16) for m in (m1, m1f, g, gt, m3))


def _pack_pairs(x):
    return pltpu.bitcast(x.astype(BF16), jnp.int32)


def _unpack_pairs(x):
    return pltpu.bitcast(x, BF16)


def _gather_rows(ref, count):
    flat = ref.reshape(count * SUBLANES, LANES)
    return jnp.concatenate([flat[pl.ds(j, count, stride=SUBLANES), :] for j in range(SUBLANES)], axis=1)


def _scatter_rows(ref, val):
    count = val.shape[0]
    flat = ref.reshape(count * SUBLANES, LANES)
    for j in range(SUBLANES):
        flat[pl.ds(j, count, stride=SUBLANES), :] = val[:, j * LANES:(j + 1) * LANES]


def _slab_specs(rows):
    return [pl.BlockSpec((rows, SUBLANES, LANES), lambda ct, jb, q=q: (0, jb, ct * DFT_SLABS + q))
            for q in range(DFT_SLABS)]


def _gather_slabs(refs):
    return jnp.concatenate([_gather_rows(r, r.shape[0]) for r in refs], axis=1)


def _stage1_kernel(m_ref, *refs):
    x_refs, o_ref, slab_refs = refs[:DFT_SLABS], refs[DFT_SLABS], refs[DFT_SLABS + 1:]
    res = jnp.dot(m_ref[...], _gather_slabs(x_refs).astype(BF16), preferred_element_type=F32)
    packed = _pack_pairs(res)
    cols = SUBLANES * LANES
    for q in range(DFT_SLABS):
        _scatter_rows(slab_refs[q], packed[:, q * cols:(q + 1) * cols])
        o_ref[:, :, q * LANES:(q + 1) * LANES] = slab_refs[q][...]


def _dft_stage1(m, x, name):
    r, k = m.shape
    _, n2, c = x.shape
    wide = DFT_SLABS * LANES
    blocks = [_nbytes((k, SUBLANES, wide), F32), _nbytes((r // 2, SUBLANES, wide), F32)]
    return pl.pallas_call(
        _stage1_kernel, grid=(c // wide, n2 // SUBLANES),
        in_specs=[pl.BlockSpec((r, k), lambda ct, jb: (0, 0))] + _slab_specs(k),
        out_specs=pl.BlockSpec((r // 2, SUBLANES, wide), lambda ct, jb: (0, jb, ct)),
        out_shape=jax.ShapeDtypeStruct((r // 2, n2, c), jnp.int32), name=name,
        scratch_shapes=[pltpu.VMEM((r // 2, SUBLANES, LANES), jnp.int32)] * DFT_SLABS,
        compiler_params=pltpu.CompilerParams(
            dimension_semantics=("parallel", "parallel"), vmem_limit_bytes=_vmem_limit(blocks, 6 * blocks[1])),
    )(m, *([x] * DFT_SLABS))


def _stage2_kernel(g_ref, gt_ref, z_ref, t_ref, o_ref):
    n1 = z_ref.shape[0]
    g = g_ref[...]
    x = jnp.dot(g, _unpack_pairs(z_ref[...]), preferred_element_type=F32)
    h = jnp.dot(g, _unpack_pairs(t_ref[...]), preferred_element_type=F32)
    xr, xi, hr, hi = x[:n1], x[n1:], h[:n1], h[n1:]
    y = jnp.concatenate([xr * hr - xi * hi, xr * hi + xi * hr], axis=0).astype(BF16)
    o_ref[...] = _pack_pairs(jnp.dot(gt_ref[...], y, preferred_element_type=F32))


def _dft_stage2(g, gt, z, tf, order):
    n1, _, c = z.shape
    blocks = [_nbytes((2 * n1, 2 * n1), BF16)] * 2 + [_nbytes((n1, c), jnp.int32)] * 3
    return pl.pallas_call(
        _stage2_kernel, grid=(n1,),
        in_specs=[pl.BlockSpec((None, 2 * n1, 2 * n1), lambda k: (k, 0, 0)),
                  pl.BlockSpec((None, 2 * n1, 2 * n1), lambda k: (k, 0, 0)),
                  pl.BlockSpec((None, n1, c), lambda k: (k, 0, 0)),
                  pl.BlockSpec((None, n1, c), lambda k: (k, 0, order))],
        out_specs=pl.BlockSpec((None, n1, c), lambda k: (k, 0, 0)),
        out_shape=jax.ShapeDtypeStruct((n1, n1, c), jnp.int32), name="dft_stage2",
        compiler_params=pltpu.CompilerParams(
            dimension_semantics=("parallel",), vmem_limit_bytes=_vmem_limit(blocks, 12 * blocks[3])),
    )(g, gt, z, tf)


def _stage3_kernel(m_ref, *refs):
    b_refs = refs[:DFT_SLABS]
    u_ref, x_ref, d_ref, o_ref = refs[DFT_SLABS:DFT_SLABS + 4]
    y_refs = refs[DFT_SLABS + 4:]
    y = jnp.dot(m_ref[...], _unpack_pairs(_gather_slabs(b_refs)), preferred_element_type=F32)
    cols = SUBLANES * LANES
    for q in range(DFT_SLABS):
        _scatter_rows(y_refs[q], y[:, q * cols:(q + 1) * cols])
        lanes = slice(q * LANES, (q + 1) * LANES)
        o_ref[:, :, lanes] = x_ref[:, :, lanes] * (y_refs[q][...] + u_ref[:, :, lanes] * d_ref[:, :, lanes])


def _dft_stage3(m3, bz, u, xg, d):
    r, k = m3.shape
    _, n2, c = u.shape
    wide = DFT_SLABS * LANES
    tile = pl.BlockSpec((r, SUBLANES, wide), lambda ct, jb: (0, jb, ct))
    blocks = [_nbytes((k // 2, SUBLANES, wide), F32)] + [_nbytes((r, SUBLANES, wide), F32)] * 3
    return pl.pallas_call(
        _stage3_kernel, grid=(c // wide, n2 // SUBLANES),
        in_specs=([pl.BlockSpec((r, k), lambda ct, jb: (0, 0))] + _slab_specs(k // 2)
                  + [tile, tile, pl.BlockSpec((1, 1, wide), lambda ct, jb: (0, 0, ct))]),
        out_specs=tile, out_shape=jax.ShapeDtypeStruct((r, n2, c), F32), name="dft_stage3",
        scratch_shapes=[pltpu.VMEM((r, SUBLANES, LANES), F32)] * DFT_SLABS,
        compiler_params=pltpu.CompilerParams(
            dimension_semantics=("parallel", "parallel"), vmem_limit_bytes=_vmem_limit(blocks, 6 * blocks[0])),
    )(m3, *([bz] * DFT_SLABS), u, xg, d.reshape(1, 1, c))


def _long_conv_gate(consts, u, xg, tf, order, d):
    m1, _, g, gt, m3 = consts
    b, l, c = u.shape
    n1 = m1.shape[1]
    u3 = u.reshape(n1, n1, c)
    z = _dft_stage1(m1, u3, "dft_stage1")
    bz = _dft_stage2(g, gt, z, tf, order)
    return _dft_stage3(m3, bz, u3, xg.reshape(n1, n1, c), d).reshape(b, l, c)


def _filter_mlp_kernel(seq_len, z_ref, t_ref, w1_ref, b1_ref, w2_ref, b2_ref, fr_ref, w3_ref, dec_ref, o_ref):
    hp = lax.Precision.HIGHEST
    fr = fr_ref[...]
    h = jnp.sin(fr * (jnp.dot(z_ref[...], w1_ref[...], precision=hp, preferred_element_type=F32) + b1_ref[...]))
    h = jnp.sin(fr * (jnp.dot(h, w2_ref[...], precision=hp, preferred_element_type=F32) + b2_ref[...]))
    h = jnp.dot(h.astype(BF16), w3_ref[...].astype(BF16), preferred_element_type=F32)
    h = h * jnp.exp(-t_ref[...] * dec_ref[...])
    tf = o_ref.shape[0]
    row = pl.program_id(0) * tf + lax.broadcasted_iota(jnp.int32, (tf, 1), 0)
    o_ref[...] = jnp.where(row == seq_len, 0.0, h).astype(o_ref.dtype)


def _hyena_taps(z_ext, t_ext, w1, b1, w2, b2, w3, freq, decay, tf=512):
    n2l, e = z_ext.shape
    seq_len = n2l // 2
    hid = w1.shape[1]
    order, _, c = decay.shape
    w3d = w3.reshape(hid, order, 2, c).transpose(2, 0, 1, 3).reshape(2, hid, order * c)
    dec = jnp.abs(decay).transpose(1, 0, 2).reshape(2, 1, order * c)
    half = seq_len // tf
    full = lambda shape: pl.BlockSpec(shape, lambda i: (0,) * len(shape))
    blocks = [_nbytes((tf, 128), F32) * 2, _nbytes((hid, order * c), F32), _nbytes((tf, order * c), F32)]
    return pl.pallas_call(
        functools.partial(_filter_mlp_kernel, seq_len), grid=(n2l // tf,),
        in_specs=[pl.BlockSpec((tf, e), lambda i: (i, 0)), pl.BlockSpec((tf, 1), lambda i: (i, 0)),
                  full((e, hid)), full((1, hid)), full((hid, hid)), full((1, hid)), full((1, hid)),
                  pl.BlockSpec((None, hid, order * c), lambda i: (i // half, 0, 0)),
                  pl.BlockSpec((None, 1, order * c), lambda i: (i // half, 0, 0))],
        out_specs=pl.BlockSpec((tf, order * c), lambda i: (i, 0)),
        out_shape=jax.ShapeDtypeStruct((n2l, order * c), F32), name="filter_taps",
        compiler_params=pltpu.CompilerParams(
            dimension_semantics=("parallel",),
            vmem_limit_bytes=_vmem_limit(blocks, 3 * _nbytes((tf, order * c), F32))),
    )(z_ext, t_ext, w1, b1.reshape(1, hid), w2, b2.reshape(1, hid), freq.reshape(1, hid), w3d, dec)


def _pos_features(seq_len, emb_dim):
    t = jnp.linspace(0.0, 1.0, seq_len, dtype=F32)[:, None]
    bands = (emb_dim - 1) // 2
    w = 2.0 * math.pi * jnp.arange(seq_len, dtype=F32)[:, None] / seq_len
    f = jnp.linspace(1e-4, bands - 1, bands, dtype=F32)[None, :]
    z = jnp.concatenate([t, jnp.cos(f * w), -jnp.sin(f * w)], axis=-1)
    pad = (-emb_dim) % 8
    z = jnp.pad(z, ((0, 0), (0, pad)))
    ext = lambda a: jnp.concatenate([a, a[:1], a[:0:-1]], axis=0)
    return ext(z), ext(t), pad


def kernel(x, ln_emb_g, ln_emb_b, w_in, conf_dw_w, conf_dw_b, conf_ln_g, conf_ln_b, sconv_w, hyena_short_w, hyena_w1, hyena_b1, hyena_w2, hyena_b2, hyena_w3, hyena_freq, hyena_decay, hyena_d, w_out, ln1_g, ln1_b, w_up, ffn_dw_w, w_down, ln2_g, ln2_b):
    b, l, d = x.shape
    depth = w_in.shape[0]
    dc = conf_dw_w.shape[2]
    ds = sconv_w.shape[2]
    dh = hyena_d.shape[2]
    dff = w_down.shape[1]
    dffp = -(-dff // 1024) * 1024
    alpha = (2.0 * depth) ** 0.25
    t = b * l
    n1 = math.isqrt(2 * l)
    assert b == 2 and n1 * n1 == 2 * l, "long conv packs two batch rows into one complex 2L = n1*n1 transform"

    consts = _dft_constants(n1)
    z_ext, t_ext, epad = _pos_features(l, hyena_w1.shape[1])

    h32, h16 = _layer_norm(x.reshape(t, d), ln_emb_g, ln_emb_b)
    for i in range(depth):
        taps = _hyena_taps(z_ext, t_ext, jnp.pad(hyena_w1[i], ((0, epad), (0, 0))), hyena_b1[i], hyena_w2[i],
                           hyena_b2[i], hyena_w3[i], hyena_freq[i], hyena_decay[i])
        oc = taps.shape[1]
        tf = _dft_stage1(consts[1], taps.reshape(n1, n1, oc), "filter_stage1")

        split = 2 * dc + 3 * ds
        assert split % MXU_COLS == 0
        wi = _cast_pad(w_in, i, 1, 1, MXU_COLS, "cast_w_in", rotate=split // MXU_COLS)
        u = _matmul(h16, wi, BF16, "mm_in").reshape(b, l, -1)
        a = _conformer_branch(u, conf_dw_w[i], conf_dw_b[i], conf_ln_g[i], conf_ln_b[i], 3 * dh // dc, dc)
        bb = _sconv_branch(u, sconv_w[i], (3 * dh + 2 * dc) // ds, ds)
        v, x1, x2 = _hyena_short_conv(u, hyena_short_w[i], 0, dh)
        zc = _long_conv_gate(consts, v, x1, tf, 0, hyena_d[i, 0])
        zc = _long_conv_gate(consts, zc, x2, tf, 1, hyena_d[i, 1])
        cat = jnp.concatenate([a, bb, zc.astype(BF16)], axis=-1).reshape(t, -1)
        mix = _matmul(cat, _cast_pad(w_out, i, 1, 1, MXU_COLS, "cast_w_out"), F32, "mm_out")
        h32, h16 = _layer_norm(mix, ln1_g[i], ln1_b[i], residual=h32, alpha=alpha)

        pad = dffp - dff
        wu = _cast_pad(w_up, i, 1, 2, MXU_COLS, "cast_w_up")
        wd = _cast_pad(w_down, i, 0, 1, MXU_COLS, "cast_w_down")
        wc = jnp.concatenate([jnp.pad(ffn_dw_w[i, :, :dff], ((0, 0), (0, pad))),
                              jnp.pad(ffn_dw_w[i, :, dff:], ((0, 0), (0, pad)))], axis=1)
        gated = _ffn_up_gate(h16, wu, wc, l)
        ffn = _matmul_ksplit(gated, wd, "mm_down", tk=dffp // 4)
        h32, h16 = _layer_norm(ffn, ln2_g[i], ln2_b[i], residual=h32, alpha=alpha)
    return h32.reshape(b, l, d)
```

```python
import functools
import math

import jax
import jax.numpy as jnp
from jax import lax
from jax.experimental import pallas as pl
from jax.experimental.pallas import tpu as pltpu

F32 = jnp.float32
BF16 = jnp.bfloat16

LN_EPS = 1e-5
CONF_WIDTH = 31
SHORT_WIDTH = 3
HALO = 16
LANES = 128
SUBLANES = 8
DFT_SLABS = 4
CONV_ROWS = 128
V7X_VMEM_BYTES = 64 * 1024 * 1024
VMEM_CAP = V7X_VMEM_BYTES - 8 * 1024 * 1024


def _vmem_limit(block_bytes, scratch_bytes=0):
    est = 2 * sum(block_bytes) + scratch_bytes
    return int(min(VMEM_CAP, max(32 * 1024 * 1024, 2 * est)))


def _nbytes(shape, dtype):
    return math.prod(shape) * jnp.dtype(dtype).itemsize


def _ln_rows(y, g, b):
    mu = jnp.mean(y, axis=-1, keepdims=True)
    d = y - mu
    var = jnp.mean(d * d, axis=-1, keepdims=True)
    return d * lax.rsqrt(var + LN_EPS) * g + b


def _ln_kernel(x_ref, g_ref, b_ref, o_ref, obf_ref):
    y = _ln_rows(x_ref[...], g_ref[...], b_ref[...])
    o_ref[...] = y
    obf_ref[...] = y.astype(BF16)


def _res_ln_kernel(alpha, h_ref, y_ref, g_ref, b_ref, o_ref, obf_ref):
    y = _ln_rows(alpha * h_ref[...] + y_ref[...].astype(F32), g_ref[...], b_ref[...])
    o_ref[...] = y
    obf_ref[...] = y.astype(BF16)


def _layer_norm(x, g, b, residual=None, alpha=1.0, tr=256):
    t, d = x.shape
    row = pl.BlockSpec((tr, d), lambda i: (i, 0))
    vec = pl.BlockSpec((1, d), lambda i: (0, 0))
    outs = (jax.ShapeDtypeStruct((t, d), F32), jax.ShapeDtypeStruct((t, d), BF16))
    blk = _nbytes((tr, d), F32)
    if residual is None:
        kern, ins, specs, nblk = _ln_kernel, (x,), [row], 3
    else:
        kern = functools.partial(_res_ln_kernel, alpha)
        ins, specs, nblk = (residual, x), [row, row], 4
    return pl.pallas_call(
        kern, grid=(t // tr,), in_specs=specs + [vec, vec], out_specs=(row, row), out_shape=outs,
        name="layer_norm" if residual is None else "residual_layer_norm",
        compiler_params=pltpu.CompilerParams(
            dimension_semantics=("parallel",), vmem_limit_bytes=_vmem_limit([blk] * nblk)),
    )(*ins, g.reshape(1, d), b.reshape(1, d))


def _mm_kernel(a_ref, b_ref, o_ref):
    o_ref[...] = jnp.dot(a_ref[...], b_ref[...], preferred_element_type=F32).astype(o_ref.dtype)


def _matmul(a, b, out_dtype, name, tm=1024, tn=1024):
    m, k = a.shape
    _, n = b.shape
    blocks = [_nbytes((tm, k), a.dtype), _nbytes((k, tn), b.dtype), _nbytes((tm, tn), out_dtype)]
    return pl.pallas_call(
        _mm_kernel, grid=(m // tm, n // tn),
        in_specs=[pl.BlockSpec((tm, k), lambda i, j: (i, 0)), pl.BlockSpec((k, tn), lambda i, j: (0, j))],
        out_specs=pl.BlockSpec((tm, tn), lambda i, j: (i, j)),
        out_shape=jax.ShapeDtypeStruct((m, n), out_dtype), name=name,
        compiler_params=pltpu.CompilerParams(
            dimension_semantics=("parallel", "parallel"),
            vmem_limit_bytes=_vmem_limit(blocks, _nbytes((tm, tn), F32))),
    )(a, b)


def _cast_pad_kernel(axis, period, nvalid, x_ref, o_ref):
    valid = pl.program_id(axis) % period < nvalid
    o_ref[...] = jnp.where(valid, x_ref[...], 0.0).astype(o_ref.dtype)


def _cast_pad(w, layer, axis, groups, blk, name, rotate=0):
    _, r, c = w.shape
    size = (r, c)[axis] // groups
    nvalid = size // blk
    period = -(-nvalid // 4) * 4
    last = groups * nvalid - 1
    assert rotate == 0 or period == nvalid

    def src(p):
        if rotate:
            return (p + rotate) % (groups * nvalid)
        return jnp.minimum(p // period * nvalid + p % period, last)

    if axis == 0:
        grid, shape = (groups * period, 1), (groups * period * blk, c)
        ispec = pl.BlockSpec((None, blk, c), lambda p, q: (layer, src(p), 0))
        ospec = pl.BlockSpec((blk, c), lambda p, q: (p, 0))
        tile = (blk, c)
    else:
        grid, shape = (1, groups * period), (r, groups * period * blk)
        ispec = pl.BlockSpec((None, r, blk), lambda q, p: (layer, 0, src(p)))
        ospec = pl.BlockSpec((r, blk), lambda q, p: (0, p))
        tile = (r, blk)
    return pl.pallas_call(
        functools.partial(_cast_pad_kernel, axis, period, nvalid), grid=grid,
        in_specs=[ispec], out_specs=ospec, out_shape=jax.ShapeDtypeStruct(shape, BF16), name=name,
        compiler_params=pltpu.CompilerParams(
            dimension_semantics=("parallel", "parallel"),
            vmem_limit_bytes=_vmem_limit([_nbytes(tile, F32), _nbytes(tile, BF16)])),
    )(w)


def _mm_ksplit_kernel(a_ref, b_ref, o_ref, acc_ref):
    @pl.when(pl.program_id(2) == 0)
    def _():
        acc_ref[...] = jnp.zeros_like(acc_ref)
    acc_ref[...] += jnp.dot(a_ref[...], b_ref[...], preferred_element_type=F32)

    @pl.when(pl.program_id(2) == pl.num_programs(2) - 1)
    def _():
        o_ref[...] = acc_ref[...].astype(o_ref.dtype)


def _matmul_ksplit(a, b, out_dtype, name, tm=1024, tn=1024, tk=2816):
    m, k = a.shape
    _, n = b.shape
    blocks = [_nbytes((tm, tk), a.dtype), _nbytes((tk, tn), b.dtype), _nbytes((tm, tn), out_dtype)]
    return pl.pallas_call(
        _mm_ksplit_kernel, grid=(m // tm, n // tn, k // tk),
        in_specs=[pl.BlockSpec((tm, tk), lambda i, j, kk: (i, kk)),
                  pl.BlockSpec((tk, tn), lambda i, j, kk: (kk, j))],
        out_specs=pl.BlockSpec((tm, tn), lambda i, j, kk: (i, j)),
        out_shape=jax.ShapeDtypeStruct((m, n), out_dtype), name=name,
        scratch_shapes=[pltpu.VMEM((tm, tn), F32)],
        compiler_params=pltpu.CompilerParams(
            dimension_semantics=("parallel", "parallel", "arbitrary"),
            vmem_limit_bytes=_vmem_limit(blocks, _nbytes((tm, tn), F32))),
    )(a, b)


def _halo_specs(seq_len, tr, cb, col_of):
    r = tr // HALO
    last = seq_len // HALO - 1

    def cur(b, i, *rest):
        return (b, i, col_of(*rest))

    def prev(b, i, *rest):
        return (b, jnp.maximum(i * r - 1, 0), col_of(*rest))

    def nxt(b, i, *rest):
        return (b, jnp.minimum((i + 1) * r, last), col_of(*rest))

    return [pl.BlockSpec((None, HALO, cb), prev), pl.BlockSpec((None, tr, cb), cur),
            pl.BlockSpec((None, HALO, cb), nxt)]


def _fill_ext(ext_ref, prev, cur, nxt):
    i = pl.program_id(1)
    tr = cur.shape[0]
    ext_ref[0:HALO, :] = jnp.where(i > 0, prev, 0.0)
    ext_ref[HALO:HALO + tr, :] = cur
    ext_ref[HALO + tr:2 * HALO + tr, :] = jnp.where(i < pl.num_programs(1) - 1, nxt, 0.0)


def _dwconv3_tile(ext_ref, w_ref, r0, rc, c0, cc):
    n = rc + 2 * SUBLANES
    lo = HALO + r0 - SUBLANES
    x = ext_ref[lo:lo + n, c0:c0 + cc]
    w = w_ref[:, c0:c0 + cc]
    y = w[0:1] * pltpu.roll(x, 1, 0) + w[1:2] * x + w[2:3] * pltpu.roll(x, n - 1, 0)
    return y[SUBLANES:SUBLANES + rc]


def _conf_kernel(vp, vc, vn, gp, gc, gn, w_ref, bias_ref, lg_ref, lb_ref, o_ref, ext_ref, sh_ref, conv_ref):
    def glu(v, g):
        return v[...].astype(F32) * jax.nn.sigmoid(g[...].astype(F32))

    _fill_ext(ext_ref, glu(vp, gp), glu(vc, gc), glu(vn, gn))
    tr, c = conv_ref.shape
    n_sh = ext_ref.shape[0] - SUBLANES
    for s in range(SUBLANES):
        sh_ref[s] = ext_ref[s:s + n_sh, :]
    first = HALO - CONF_WIDTH // 2
    for c0 in range(0, c, LANES):
        for r0 in range(0, tr, 64):
            acc = None
            for k in range(CONF_WIDTH):
                a, s = divmod(first + k, SUBLANES)
                term = w_ref[k:k + 1, c0:c0 + LANES] * sh_ref[s, SUBLANES * a + r0:SUBLANES * a + r0 + 64, c0:c0 + LANES]
                acc = term if acc is None else acc + term
            conv_ref[r0:r0 + 64, c0:c0 + LANES] = acc
    for r0 in range(0, tr, 16):
        y = _ln_rows(conv_ref[r0:r0 + 16, :] + bias_ref[...], lg_ref[...], lb_ref[...])
        o_ref[r0:r0 + 16, :] = (y * jax.nn.sigmoid(y)).astype(o_ref.dtype)


def _conformer_branch(u, w, bias, ln_g, ln_b, col0, dc, tr=256):
    b, l, _ = u.shape
    specs = _halo_specs(l, tr, dc, lambda: col0) + _halo_specs(l, tr, dc, lambda: col0 + 1)
    vec = pl.BlockSpec((1, dc), lambda bb, i: (0, 0))
    blocks = [_nbytes((tr + 2 * HALO, dc), F32)] * 2 + [_nbytes((CONF_WIDTH, dc), F32), _nbytes((tr, dc), BF16)]
    n_sh = tr + 2 * HALO - SUBLANES
    scratch = _nbytes((tr + 2 * HALO, dc), F32) + _nbytes((tr, dc), F32) + _nbytes((SUBLANES, n_sh, dc), F32)
    return pl.pallas_call(
        _conf_kernel, grid=(b, l // tr),
        in_specs=specs + [pl.BlockSpec((CONF_WIDTH, dc), lambda bb, i: (0, 0)), vec, vec, vec],
        out_specs=pl.BlockSpec((None, tr, dc), lambda bb, i: (bb, i, 0)),
        out_shape=jax.ShapeDtypeStruct((b, l, dc), BF16), name="conformer_branch",
        scratch_shapes=[pltpu.VMEM((tr + 2 * HALO, dc), F32), pltpu.VMEM((SUBLANES, n_sh, dc), F32),
                        pltpu.VMEM((tr, dc), F32)],
        compiler_params=pltpu.CompilerParams(
            dimension_semantics=("parallel", "parallel"), vmem_limit_bytes=_vmem_limit(blocks, scratch)),
    )(u, u, u, u, u, u, w, bias.reshape(1, dc), ln_g.reshape(1, dc), ln_b.reshape(1, dc))


def _sconv_kernel(gb_ref, cp, cc, cn, xp, xc, xn, w_ref, o_ref, ext_ref):
    prod = lambda a, b: a[...].astype(F32) * b[...].astype(F32)
    _fill_ext(ext_ref, prod(cp, xp), prod(cc, xc), prod(cn, xn))
    tr, c = o_ref.shape
    for c0 in range(0, c, 256):
        for r0 in range(0, tr, CONV_ROWS):
            y = _dwconv3_tile(ext_ref, w_ref, r0, CONV_ROWS, c0, 256)
            o_ref[r0:r0 + CONV_ROWS, c0:c0 + 256] = (gb_ref[r0:r0 + CONV_ROWS, c0:c0 + 256].astype(F32) * y).astype(o_ref.dtype)


def _sconv_branch(u, w, col0, ds, tr=256):
    b, l, _ = u.shape
    specs = ([pl.BlockSpec((None, tr, ds), lambda bb, i: (bb, i, col0))]
             + _halo_specs(l, tr, ds, lambda: col0 + 1) + _halo_specs(l, tr, ds, lambda: col0 + 2))
    blocks = [_nbytes((tr + 2 * HALO, ds), F32)] * 3 + [_nbytes((tr, ds), BF16)]
    return pl.pallas_call(
        _sconv_kernel, grid=(b, l // tr),
        in_specs=specs + [pl.BlockSpec((SHORT_WIDTH, ds), lambda bb, i: (0, 0))],
        out_specs=pl.BlockSpec((None, tr, ds), lambda bb, i: (bb, i, 0)),
        out_shape=jax.ShapeDtypeStruct((b, l, ds), BF16), name="sconv_branch",
        scratch_shapes=[pltpu.VMEM((tr + 2 * HALO, ds), F32)],
        compiler_params=pltpu.CompilerParams(
            dimension_semantics=("parallel", "parallel"),
            vmem_limit_bytes=_vmem_limit(blocks, _nbytes((tr + 2 * HALO, ds), F32))),
    )(u, u, u, u, u, u, u, w)


def _hyena_short_kernel(xp, xc, xn, w_ref, v_ref, x1_ref, x2_ref, ext_ref):
    _fill_ext(ext_ref, xp[...].astype(F32), xc[...].astype(F32), xn[...].astype(F32))
    tr, dh = v_ref.shape
    for part, o_ref in enumerate((v_ref, x1_ref, x2_ref)):
        for c0 in range(0, dh, 256):
            for r0 in range(0, tr, CONV_ROWS):
                o_ref[r0:r0 + CONV_ROWS, c0:c0 + 256] = _dwconv3_tile(
                    ext_ref, w_ref, r0, CONV_ROWS, part * dh + c0, 256)


def _hyena_short_conv(u, w, col0, dh, tr=256):
    b, l, _ = u.shape
    out = jax.ShapeDtypeStruct((b, l, dh), F32)
    ospec = pl.BlockSpec((None, tr, dh), lambda bb, i: (bb, i, 0))
    blocks = [_nbytes((tr + 2 * HALO, 3 * dh), F32)] + [_nbytes((tr, dh), F32)] * 3
    return pl.pallas_call(
        _hyena_short_kernel, grid=(b, l // tr),
        in_specs=(_halo_specs(l, tr, 3 * dh, lambda: col0)
                  + [pl.BlockSpec((SHORT_WIDTH, 3 * dh), lambda bb, i: (0, 0))]),
        out_specs=(ospec, ospec, ospec), out_shape=(out, out, out), name="hyena_short_conv",
        scratch_shapes=[pltpu.VMEM((tr + 2 * HALO, 3 * dh), F32)],
        compiler_params=pltpu.CompilerParams(
            dimension_semantics=("parallel", "parallel"),
            vmem_limit_bytes=_vmem_limit(blocks, _nbytes((tr + 2 * HALO, 3 * dh), F32))),
    )(u, u, u, w)


MXU_COLS = 256
GATE_ROWS = 256


def _ffn_up_gate_kernel(tiles_per_seq, ap_ref, ac_ref, an_ref, wg_ref, wv_ref, cg_ref, cv_ref, o_ref, a_ext,
                        up_ref):
    tm, tn = o_ref.shape
    rows = tm + 2 * HALO

    @pl.when(pl.program_id(1) == 0)
    def _():
        t = pl.program_id(0) % tiles_per_seq
        a_ext[0:HALO, :] = jnp.where(t > 0, ap_ref[...], jnp.zeros_like(ap_ref))
        a_ext[HALO:HALO + tm, :] = ac_ref[...]
        a_ext[HALO + tm:, :] = jnp.where(t < tiles_per_seq - 1, an_ref[...], jnp.zeros_like(an_ref))

    def conv(x, c_ref, c0):
        n = x.shape[0]
        w = c_ref[:, c0:c0 + MXU_COLS]
        y = w[0:1] * pltpu.roll(x, 1, 0) + w[1:2] * x + w[2:3] * pltpu.roll(x, n - 1, 0)
        return y[SUBLANES:n - SUBLANES]

    for s, c0 in enumerate(range(0, tn, MXU_COLS)):
        a = a_ext[...]
        up_ref[0, s] = jnp.dot(a, wg_ref[:, c0:c0 + MXU_COLS], preferred_element_type=F32)
        up_ref[1, s] = jnp.dot(a, wv_ref[:, c0:c0 + MXU_COLS], preferred_element_type=F32)
        for r0 in range(0, tm, GATE_ROWS):
            lo, hi = HALO + r0 - SUBLANES, HALO + r0 + GATE_ROWS + SUBLANES
            g = conv(up_ref[0, s, lo:hi, :], cg_ref, c0)
            v = conv(up_ref[1, s, lo:hi, :], cv_ref, c0)
            hg = 0.5 * g
            o_ref[r0:r0 + GATE_ROWS, c0:c0 + MXU_COLS] = ((hg + hg * jnp.tanh(hg)) * v).astype(o_ref.dtype)


def _ffn_up_gate(h, wu, wc, seq_len, tm=1024, tn=512):
    t, d = h.shape
    f = wu.shape[1] // 2
    nc = f // tn
    r = tm // HALO
    last = t // HALO - 1
    specs = [pl.BlockSpec((HALO, d), lambda i, j: (jnp.maximum(i * r - 1, 0), 0)),
             pl.BlockSpec((tm, d), lambda i, j: (i, 0)),
             pl.BlockSpec((HALO, d), lambda i, j: (jnp.minimum((i + 1) * r, last), 0)),
             pl.BlockSpec((d, tn), lambda i, j: (0, j)), pl.BlockSpec((d, tn), lambda i, j: (0, j + nc)),
             pl.BlockSpec((SHORT_WIDTH, tn), lambda i, j: (0, j)),
             pl.BlockSpec((SHORT_WIDTH, tn), lambda i, j: (0, j + nc))]
    blocks = [_nbytes((tm + 2 * HALO, d), BF16), 2 * _nbytes((d, tn), BF16), _nbytes((tm, tn), BF16)]
    scratch = _nbytes((tm + 2 * HALO, d), BF16) + 4 * _nbytes((tm + 2 * HALO, MXU_COLS), F32)
    return pl.pallas_call(
        functools.partial(_ffn_up_gate_kernel, seq_len // tm), grid=(t // tm, nc),
        in_specs=specs, out_specs=pl.BlockSpec((tm, tn), lambda i, j: (i, j)),
        out_shape=jax.ShapeDtypeStruct((t, f), BF16), name="ffn_up_gate",
        scratch_shapes=[pltpu.VMEM((tm + 2 * HALO, d), BF16),
                        pltpu.VMEM((2, tn // MXU_COLS, tm + 2 * HALO, MXU_COLS), F32)],
        compiler_params=pltpu.CompilerParams(
            dimension_semantics=("parallel", "arbitrary"), vmem_limit_bytes=_vmem_limit(blocks, scratch)),
    )(h, h, h, wu, wu, wc, wc)


def _dft_constants(n1):
    n = n1 * n1
    h = n1 // 2
    idx = jnp.arange(n1, dtype=jnp.int32)
    ang1 = (2.0 * math.pi / n1) * ((idx[:, None] * idx[None, :]) % n1).astype(F32)
    c1, s1 = jnp.cos(ang1), jnp.sin(ang1)
    m1 = jnp.stack([jnp.concatenate([c1[:, :h], s1[:, :h]], axis=1),
                    jnp.concatenate([-s1[:, :h], c1[:, :h]], axis=1)], axis=1).reshape(2 * n1, n1)
    m1f = jnp.stack([c1, -s1], axis=1).reshape(2 * n1, n1)
    ct, st = c1[:h, :], s1[:h, :]
    m3 = jnp.concatenate([jnp.stack([ct, -st], axis=2).reshape(h, 2 * n1),
                          jnp.stack([st, ct], axis=2).reshape(h, 2 * n1)], axis=0) / n
    k = idx[:, None, None] + n1 * idx[None, :, None]
    ang2 = (2.0 * math.pi / n) * ((k * idx[None, None, :]) % n).astype(F32)
    c2, s2 = jnp.cos(ang2), jnp.sin(ang2)
    g = jnp.concatenate([jnp.concatenate([c2, s2], axis=2), jnp.concatenate([-s2, c2], axis=2)], axis=1)
    gt = jnp.swapaxes(g, 1, 2)
    g = g.reshape(n1, 2 * n1, 2, n1).swapaxes(2, 3).reshape(n1, 2 * n1, 2 * n1)
    gt = gt.reshape(n1, 2, n1, 2 * n1).swapaxes(1, 2).reshape(n1, 2 * n1, 2 * n1)
    return tuple(m.astype(BF16) for m in (m1, m1f, g, gt, m3))


def _pack_pairs(x):
    return pltpu.bitcast(x.astype(BF16), jnp.int32)


def _unpack_pairs(x):
    return pltpu.bitcast(x, BF16)


def _gather_rows(ref, count):
    flat = ref.reshape(count * SUBLANES, LANES)
    return jnp.concatenate([flat[pl.ds(j, count, stride=SUBLANES), :] for j in range(SUBLANES)], axis=1)


def _scatter_rows(ref, val):
    count = val.shape[0]
    flat = ref.reshape(count * SUBLANES, LANES)
    for j in range(SUBLANES):
        flat[pl.ds(j, count, stride=SUBLANES), :] = val[:, j * LANES:(j + 1) * LANES]


def _slab_specs(rows):
    return [pl.BlockSpec((rows, SUBLANES, LANES), lambda ct, jb, q=q: (0, jb, ct * DFT_SLABS + q))
            for q in range(DFT_SLABS)]


def _gather_slabs(refs):
    return jnp.concatenate([_gather_rows(r, r.shape[0]) for r in refs], axis=1)


def _stage1_kernel(m_ref, *refs):
    x_refs, o_ref, slab_refs = refs[:DFT_SLABS], refs[DFT_SLABS], refs[DFT_SLABS + 1:]
    res = jnp.dot(m_ref[...], _gather_slabs(x_refs).astype(BF16), preferred_element_type=F32)
    packed = _pack_pairs(res)
    cols = SUBLANES * LANES
    for q in range(DFT_SLABS):
        _scatter_rows(slab_refs[q], packed[:, q * cols:(q + 1) * cols])
        o_ref[:, :, q * LANES:(q + 1) * LANES] = slab_refs[q][...]


def _dft_stage1(m, x, name):
    r, k = m.shape
    _, n2, c = x.shape
    wide = DFT_SLABS * LANES
    blocks = [_nbytes((k, SUBLANES, wide), F32), _nbytes((r // 2, SUBLANES, wide), F32)]
    return pl.pallas_call(
        _stage1_kernel, grid=(c // wide, n2 // SUBLANES),
        in_specs=[pl.BlockSpec((r, k), lambda ct, jb: (0, 0))] + _slab_specs(k),
        out_specs=pl.BlockSpec((r // 2, SUBLANES, wide), lambda ct, jb: (0, jb, ct)),
        out_shape=jax.ShapeDtypeStruct((r // 2, n2, c), jnp.int32), name=name,
        scratch_shapes=[pltpu.VMEM((r // 2, SUBLANES, LANES), jnp.int32)] * DFT_SLABS,
        compiler_params=pltpu.CompilerParams(
            dimension_semantics=("parallel", "parallel"), vmem_limit_bytes=_vmem_limit(blocks, 6 * blocks[1])),
    )(m, *([x] * DFT_SLABS))


def _stage2_kernel(g_ref, gt_ref, z_ref, t_ref, o_ref):
    nk, n1, _ = z_ref.shape
    for k in range(nk):
        g = g_ref[k]
        x = jnp.dot(g, _unpack_pairs(z_ref[k]), preferred_element_type=F32)
        h = jnp.dot(g, _unpack_pairs(t_ref[k]), preferred_element_type=F32)
        xr, xi, hr, hi = x[:n1], x[n1:], h[:n1], h[n1:]
        y = jnp.concatenate([xr * hr - xi * hi, xr * hi + xi * hr], axis=0).astype(BF16)
        o_ref[k] = _pack_pairs(jnp.dot(gt_ref[k], y, preferred_element_type=F32))


def _dft_stage2(g, gt, z, tf, order, nk=2):
    n1, _, c = z.shape
    blocks = [_nbytes((nk, 2 * n1, 2 * n1), BF16)] * 2 + [_nbytes((nk, n1, c), jnp.int32)] * 3
    return pl.pallas_call(
        _stage2_kernel, grid=(n1 // nk,),
        in_specs=[pl.BlockSpec((nk, 2 * n1, 2 * n1), lambda k: (k, 0, 0)),
                  pl.BlockSpec((nk, 2 * n1, 2 * n1), lambda k: (k, 0, 0)),
                  pl.BlockSpec((nk, n1, c), lambda k: (k, 0, 0)),
                  pl.BlockSpec((nk, n1, c), lambda k: (k, 0, order))],
        out_specs=pl.BlockSpec((nk, n1, c), lambda k: (k, 0, 0)),
        out_shape=jax.ShapeDtypeStruct((n1, n1, c), jnp.int32), name="dft_stage2",
        compiler_params=pltpu.CompilerParams(
            dimension_semantics=("parallel",), vmem_limit_bytes=_vmem_limit(blocks, 6 * blocks[3])),
    )(g, gt, z, tf)


def _stage3_kernel(m_ref, *refs):
    b_refs = refs[:DFT_SLABS]
    u_ref, x_ref, d_ref, o_ref = refs[DFT_SLABS:DFT_SLABS + 4]
    y_refs = refs[DFT_SLABS + 4:]
    y = jnp.dot(m_ref[...], _unpack_pairs(_gather_slabs(b_refs)), preferred_element_type=F32)
    cols = SUBLANES * LANES
    for q in range(DFT_SLABS):
        _scatter_rows(y_refs[q], y[:, q * cols:(q + 1) * cols])
        lanes = slice(q * LANES, (q + 1) * LANES)
        o_ref[:, :, lanes] = x_ref[:, :, lanes] * (y_refs[q][...] + u_ref[:, :, lanes] * d_ref[:, :, lanes])


def _dft_stage3(m3, bz, u, xg, d):
    r, k = m3.shape
    _, n2, c = u.shape
    wide = DFT_SLABS * LANES
    tile = pl.BlockSpec((r, SUBLANES, wide), lambda ct, jb: (0, jb, ct))
    blocks = [_nbytes((k // 2, SUBLANES, wide), F32)] + [_nbytes((r, SUBLANES, wide), F32)] * 3
    return pl.pallas_call(
        _stage3_kernel, grid=(c // wide, n2 // SUBLANES),
        in_specs=([pl.BlockSpec((r, k), lambda ct, jb: (0, 0))] + _slab_specs(k // 2)
                  + [tile, tile, pl.BlockSpec((1, 1, wide), lambda ct, jb: (0, 0, ct))]),
        out_specs=tile, out_shape=jax.ShapeDtypeStruct((r, n2, c), F32), name="dft_stage3",
        scratch_shapes=[pltpu.VMEM((r, SUBLANES, LANES), F32)] * DFT_SLABS,
        compiler_params=pltpu.CompilerParams(
            dimension_semantics=("parallel", "parallel"), vmem_limit_bytes=_vmem_limit(blocks, 6 * blocks[0])),
    )(m3, *([bz] * DFT_SLABS), u, xg, d.reshape(1, 1, c))


def _long_conv_gate(consts, u, xg, tf, order, d):
    m1, _, g, gt, m3 = consts
    b, l, c = u.shape
    n1 = m1.shape[1]
    u3 = u.reshape(n1, n1, c)
    z = _dft_stage1(m1, u3, "dft_stage1")
    bz = _dft_stage2(g, gt, z, tf, order)
    return _dft_stage3(m3, bz, u3, xg.reshape(n1, n1, c), d).reshape(b, l, c)


def _filter_mlp_kernel(seq_len, z_ref, t_ref, w1_ref, b1_ref, w2_ref, b2_ref, fr_ref, w3_ref, dec_ref, o_ref):
    hp = lax.Precision.HIGHEST
    fr = fr_ref[...]
    h = jnp.sin(fr * (jnp.dot(z_ref[...], w1_ref[...], precision=hp, preferred_element_type=F32) + b1_ref[...]))
    h = jnp.sin(fr * (jnp.dot(h, w2_ref[...], precision=hp, preferred_element_type=F32) + b2_ref[...]))
    h = jnp.dot(h.astype(BF16), w3_ref[...].astype(BF16), preferred_element_type=F32)
    h = h * jnp.exp(-t_ref[...] * dec_ref[...])
    tf = o_ref.shape[0]
    row = pl.program_id(0) * tf + lax.broadcasted_iota(jnp.int32, (tf, 1), 0)
    o_ref[...] = jnp.where(row == seq_len, 0.0, h).astype(o_ref.dtype)


def _hyena_taps(z_ext, t_ext, w1, b1, w2, b2, w3, freq, decay, tf=512):
    n2l, e = z_ext.shape
    seq_len = n2l // 2
    hid = w1.shape[1]
    order, _, c = decay.shape
    w3d = w3.reshape(hid, order, 2, c).transpose(2, 0, 1, 3).reshape(2, hid, order * c)
    dec = jnp.abs(decay).transpose(1, 0, 2).reshape(2, 1, order * c)
    half = seq_len // tf
    full = lambda shape: pl.BlockSpec(shape, lambda i: (0,) * len(shape))
    blocks = [_nbytes((tf, 128), F32) * 2, _nbytes((hid, order * c), F32), _nbytes((tf, order * c), F32)]
    return pl.pallas_call(
        functools.partial(_filter_mlp_kernel, seq_len), grid=(n2l // tf,),
        in_specs=[pl.BlockSpec((tf, e), lambda i: (i, 0)), pl.BlockSpec((tf, 1), lambda i: (i, 0)),
                  full((e, hid)), full((1, hid)), full((hid, hid)), full((1, hid)), full((1, hid)),
                  pl.BlockSpec((None, hid, order * c), lambda i: (i // half, 0, 0)),
                  pl.BlockSpec((None, 1, order * c), lambda i: (i // half, 0, 0))],
        out_specs=pl.BlockSpec((tf, order * c), lambda i: (i, 0)),
        out_shape=jax.ShapeDtypeStruct((n2l, order * c), F32), name="filter_taps",
        compiler_params=pltpu.CompilerParams(
            dimension_semantics=("parallel",),
            vmem_limit_bytes=_vmem_limit(blocks, 3 * _nbytes((tf, order * c), F32))),
    )(z_ext, t_ext, w1, b1.reshape(1, hid), w2, b2.reshape(1, hid), freq.reshape(1, hid), w3d, dec)


def _pos_features(seq_len, emb_dim):
    t = jnp.linspace(0.0, 1.0, seq_len, dtype=F32)[:, None]
    bands = (emb_dim - 1) // 2
    w = 2.0 * math.pi * jnp.arange(seq_len, dtype=F32)[:, None] / seq_len
    f = jnp.linspace(1e-4, bands - 1, bands, dtype=F32)[None, :]
    z = jnp.concatenate([t, jnp.cos(f * w), -jnp.sin(f * w)], axis=-1)
    pad = (-emb_dim) % 8
    z = jnp.pad(z, ((0, 0), (0, pad)))
    ext = lambda a: jnp.concatenate([a, a[:1], a[:0:-1]], axis=0)
    return ext(z), ext(t), pad


def kernel(x, ln_emb_g, ln_emb_b, w_in, conf_dw_w, conf_dw_b, conf_ln_g, conf_ln_b, sconv_w, hyena_short_w, hyena_w1, hyena_b1, hyena_w2, hyena_b2, hyena_w3, hyena_freq, hyena_decay, hyena_d, w_out, ln1_g, ln1_b, w_up, ffn_dw_w, w_down, ln2_g, ln2_b):
    b, l, d = x.shape
    depth = w_in.shape[0]
    dc = conf_dw_w.shape[2]
    ds = sconv_w.shape[2]
    dh = hyena_d.shape[2]
    dff = w_down.shape[1]
    dffp = -(-dff // 1024) * 1024
    alpha = (2.0 * depth) ** 0.25
    t = b * l
    n1 = math.isqrt(2 * l)
    assert b == 2 and n1 * n1 == 2 * l, "long conv packs two batch rows into one complex 2L = n1*n1 transform"

    consts = _dft_constants(n1)
    z_ext, t_ext, epad = _pos_features(l, hyena_w1.shape[1])

    h32, h16 = _layer_norm(x.reshape(t, d), ln_emb_g, ln_emb_b)
    for i in range(depth):
        taps = _hyena_taps(z_ext, t_ext, jnp.pad(hyena_w1[i], ((0, epad), (0, 0))), hyena_b1[i], hyena_w2[i],
                           hyena_b2[i], hyena_w3[i], hyena_freq[i], hyena_decay[i])
        oc = taps.shape[1]
        tf = _dft_stage1(consts[1], taps.reshape(n1, n1, oc), "filter_stage1")

        split = 2 * dc + 3 * ds
        assert split % MXU_COLS == 0
        wi = _cast_pad(w_in, i, 1, 1, MXU_COLS, "cast_w_in", rotate=split // MXU_COLS)
        u = _matmul(h16, wi, BF16, "mm_in").reshape(b, l, -1)
        a = _conformer_branch(u, conf_dw_w[i], conf_dw_b[i], conf_ln_g[i], conf_ln_b[i], 3 * dh // dc, dc)
        bb = _sconv_branch(u, sconv_w[i], (3 * dh + 2 * dc) // ds, ds)
        v, x1, x2 = _hyena_short_conv(u, hyena_short_w[i], 0, dh)
        zc = _long_conv_gate(consts, v, x1, tf, 0, hyena_d[i, 0])
        zc = _long_conv_gate(consts, zc, x2, tf, 1, hyena_d[i, 1])
        cat = jnp.concatenate([a, bb, zc.astype(BF16)], axis=-1).reshape(t, -1)
        mix = _matmul(cat, _cast_pad(w_out, i, 1, 1, MXU_COLS, "cast_w_out"), BF16, "mm_out")
        h32, h16 = _layer_norm(mix, ln1_g[i], ln1_b[i], residual=h32, alpha=alpha)

        pad = dffp - dff
        wu = _cast_pad(w_up, i, 1, 2, MXU_COLS, "cast_w_up")
        wd = _cast_pad(w_down, i, 0, 1, MXU_COLS, "cast_w_down")
        wc = jnp.concatenate([jnp.pad(ffn_dw_w[i, :, :dff], ((0, 0), (0, pad))),
                              jnp.pad(ffn_dw_w[i, :, dff:], ((0, 0), (0, pad)))], axis=1)
        gated = _ffn_up_gate(h16, wu, wc, l)
        ffn = _matmul_ksplit(gated, wd, BF16, "mm_down", tk=dffp // 4)
        h32, h16 = _layer_norm(ffn, ln2_g[i], ln2_b[i], residual=h32, alpha=alpha)
    return h32.reshape(b, l, d)
```

```python
import functools
import math

import jax
import jax.numpy as jnp
from jax import lax
from jax.experimental import pallas as pl
from jax.experimental.pallas import tpu as pltpu

F32 = jnp.float32
BF16 = jnp.bfloat16

LN_EPS = 1e-5
CONF_WIDTH = 31
SHORT_WIDTH = 3
HALO = 16
LANES = 128
SUBLANES = 8
DFT_SLABS = 8
CONV_ROWS = 128
V7X_VMEM_BYTES = 64 * 1024 * 1024
VMEM_CAP = V7X_VMEM_BYTES - 8 * 1024 * 1024


def _vmem_limit(block_bytes, scratch_bytes=0):
    est = 2 * sum(block_bytes) + scratch_bytes
    return int(min(VMEM_CAP, max(32 * 1024 * 1024, 2 * est)))


def _nbytes(shape, dtype):
    return math.prod(shape) * jnp.dtype(dtype).itemsize


def _ln_rows(y, g, b):
    mu = jnp.mean(y, axis=-1, keepdims=True)
    d = y - mu
    var = jnp.mean(d * d, axis=-1, keepdims=True)
    return d * lax.rsqrt(var + LN_EPS) * g + b


def _ln_kernel(x_ref, g_ref, b_ref, o_ref, obf_ref):
    y = _ln_rows(x_ref[...], g_ref[...], b_ref[...])
    o_ref[...] = y
    obf_ref[...] = y.astype(BF16)


def _res_ln_kernel(alpha, h_ref, y_ref, g_ref, b_ref, o_ref, obf_ref):
    y = _ln_rows(alpha * h_ref[...] + y_ref[...].astype(F32), g_ref[...], b_ref[...])
    o_ref[...] = y
    obf_ref[...] = y.astype(BF16)


def _layer_norm(x, g, b, residual=None, alpha=1.0, tr=256):
    t, d = x.shape
    row = pl.BlockSpec((tr, d), lambda i: (i, 0))
    vec = pl.BlockSpec((1, d), lambda i: (0, 0))
    outs = (jax.ShapeDtypeStruct((t, d), F32), jax.ShapeDtypeStruct((t, d), BF16))
    blk = _nbytes((tr, d), F32)
    if residual is None:
        kern, ins, specs, nblk = _ln_kernel, (x,), [row], 3
    else:
        kern = functools.partial(_res_ln_kernel, alpha)
        ins, specs, nblk = (residual, x), [row, row], 4
    return pl.pallas_call(
        kern, grid=(t // tr,), in_specs=specs + [vec, vec], out_specs=(row, row), out_shape=outs,
        name="layer_norm" if residual is None else "residual_layer_norm",
        compiler_params=pltpu.CompilerParams(
            dimension_semantics=("parallel",), vmem_limit_bytes=_vmem_limit([blk] * nblk)),
    )(*ins, g.reshape(1, d), b.reshape(1, d))


def _mm_kernel(a_ref, b_ref, o_ref):
    o_ref[...] = jnp.dot(a_ref[...], b_ref[...], preferred_element_type=F32).astype(o_ref.dtype)


def _matmul(a, b, out_dtype, name, tm=1024, tn=1024):
    m, k = a.shape
    _, n = b.shape
    blocks = [_nbytes((tm, k), a.dtype), _nbytes((k, tn), b.dtype), _nbytes((tm, tn), out_dtype)]
    return pl.pallas_call(
        _mm_kernel, grid=(m // tm, n // tn),
        in_specs=[pl.BlockSpec((tm, k), lambda i, j: (i, 0)), pl.BlockSpec((k, tn), lambda i, j: (0, j))],
        out_specs=pl.BlockSpec((tm, tn), lambda i, j: (i, j)),
        out_shape=jax.ShapeDtypeStruct((m, n), out_dtype), name=name,
        compiler_params=pltpu.CompilerParams(
            dimension_semantics=("parallel", "parallel"),
            vmem_limit_bytes=_vmem_limit(blocks, _nbytes((tm, tn), F32))),
    )(a, b)


def _mm_cat_kernel(a_ref, b_ref, z_ref, w_ref, o_ref, cat_ref):
    @pl.when(pl.program_id(1) == 0)
    def _():
        da, db = a_ref.shape[1], b_ref.shape[1]
        cat_ref[:, 0:da] = a_ref[...]
        cat_ref[:, da:da + db] = b_ref[...]
        cat_ref[:, da + db:] = z_ref[...].astype(BF16)
    o_ref[...] = jnp.dot(cat_ref[...], w_ref[...], preferred_element_type=F32).astype(o_ref.dtype)


def _matmul_cat(a, b, z, w, out_dtype, name, tm=1024, tn=512):
    m, da = a.shape
    db, dz = b.shape[1], z.shape[1]
    k, n = w.shape
    assert da + db + dz == k
    blocks = [_nbytes((tm, da + db), BF16), _nbytes((tm, dz), F32), _nbytes((k, tn), BF16),
              _nbytes((tm, tn), out_dtype)]
    row = lambda width: pl.BlockSpec((tm, width), lambda i, j: (i, 0))
    return pl.pallas_call(
        _mm_cat_kernel, grid=(m // tm, n // tn),
        in_specs=[row(da), row(db), row(dz), pl.BlockSpec((k, tn), lambda i, j: (0, j))],
        out_specs=pl.BlockSpec((tm, tn), lambda i, j: (i, j)),
        out_shape=jax.ShapeDtypeStruct((m, n), out_dtype), name=name,
        scratch_shapes=[pltpu.VMEM((tm, k), BF16)],
        compiler_params=pltpu.CompilerParams(
            dimension_semantics=("parallel", "arbitrary"),
            vmem_limit_bytes=_vmem_limit(blocks, _nbytes((tm, k), BF16) + _nbytes((tm, tn), F32))),
    )(a, b, z, w)


def _cast_pad_kernel(axis, period, nvalid, x_ref, o_ref):
    valid = pl.program_id(axis) % period < nvalid
    o_ref[...] = jnp.where(valid, x_ref[...], 0.0).astype(o_ref.dtype)


def _cast_pad(w, layer, axis, groups, blk, name, rotate=0):
    _, r, c = w.shape
    size = (r, c)[axis] // groups
    nvalid = size // blk
    period = -(-nvalid // 4) * 4
    last = groups * nvalid - 1
    assert rotate == 0 or period == nvalid

    def src(p):
        if rotate:
            return (p + rotate) % (groups * nvalid)
        return jnp.minimum(p // period * nvalid + p % period, last)

    if axis == 0:
        grid, shape = (groups * period, 1), (groups * period * blk, c)
        ispec = pl.BlockSpec((None, blk, c), lambda p, q: (layer, src(p), 0))
        ospec = pl.BlockSpec((blk, c), lambda p, q: (p, 0))
        tile = (blk, c)
    else:
        grid, shape = (1, groups * period), (r, groups * period * blk)
        ispec = pl.BlockSpec((None, r, blk), lambda q, p: (layer, 0, src(p)))
        ospec = pl.BlockSpec((r, blk), lambda q, p: (0, p))
        tile = (r, blk)
    return pl.pallas_call(
        functools.partial(_cast_pad_kernel, axis, period, nvalid), grid=grid,
        in_specs=[ispec], out_specs=ospec, out_shape=jax.ShapeDtypeStruct(shape, BF16), name=name,
        compiler_params=pltpu.CompilerParams(
            dimension_semantics=("parallel", "parallel"),
            vmem_limit_bytes=_vmem_limit([_nbytes(tile, F32), _nbytes(tile, BF16)])),
    )(w)


def _mm_ksplit_kernel(a_ref, b_ref, o_ref, acc_ref):
    @pl.when(pl.program_id(2) == 0)
    def _():
        acc_ref[...] = jnp.zeros_like(acc_ref)
    acc_ref[...] += jnp.dot(a_ref[...], b_ref[...], preferred_element_type=F32)

    @pl.when(pl.program_id(2) == pl.num_programs(2) - 1)
    def _():
        o_ref[...] = acc_ref[...].astype(o_ref.dtype)


def _matmul_ksplit(a, b, out_dtype, name, tm=1024, tn=1024, tk=2816):
    m, k = a.shape
    _, n = b.shape
    blocks = [_nbytes((tm, tk), a.dtype), _nbytes((tk, tn), b.dtype), _nbytes((tm, tn), out_dtype)]
    return pl.pallas_call(
        _mm_ksplit_kernel, grid=(m // tm, n // tn, k // tk),
        in_specs=[pl.BlockSpec((tm, tk), lambda i, j, kk: (i, kk)),
                  pl.BlockSpec((tk, tn), lambda i, j, kk: (kk, j))],
        out_specs=pl.BlockSpec((tm, tn), lambda i, j, kk: (i, j)),
        out_shape=jax.ShapeDtypeStruct((m, n), out_dtype), name=name,
        scratch_shapes=[pltpu.VMEM((tm, tn), F32)],
        compiler_params=pltpu.CompilerParams(
            dimension_semantics=("parallel", "parallel", "arbitrary"),
            vmem_limit_bytes=_vmem_limit(blocks, _nbytes((tm, tn), F32))),
    )(a, b)


def _halo_specs(seq_len, tr, cb, col_of):
    r = tr // HALO
    last = seq_len // HALO - 1

    def cur(b, i, *rest):
        return (b, i, col_of(*rest))

    def prev(b, i, *rest):
        return (b, jnp.maximum(i * r - 1, 0), col_of(*rest))

    def nxt(b, i, *rest):
        return (b, jnp.minimum((i + 1) * r, last), col_of(*rest))

    return [pl.BlockSpec((None, HALO, cb), prev), pl.BlockSpec((None, tr, cb), cur),
            pl.BlockSpec((None, HALO, cb), nxt)]


def _fill_ext(ext_ref, prev, cur, nxt):
    i = pl.program_id(1)
    tr = cur.shape[0]
    ext_ref[0:HALO, :] = jnp.where(i > 0, prev, 0.0)
    ext_ref[HALO:HALO + tr, :] = cur
    ext_ref[HALO + tr:2 * HALO + tr, :] = jnp.where(i < pl.num_programs(1) - 1, nxt, 0.0)


def _dwconv3_tile(ext_ref, w_ref, r0, rc, c0, cc):
    n = rc + 2 * SUBLANES
    lo = HALO + r0 - SUBLANES
    x = ext_ref[lo:lo + n, c0:c0 + cc]
    w = w_ref[:, c0:c0 + cc]
    y = w[0:1] * pltpu.roll(x, 1, 0) + w[1:2] * x + w[2:3] * pltpu.roll(x, n - 1, 0)
    return y[SUBLANES:SUBLANES + rc]


def _conf_kernel(vp, vc, vn, gp, gc, gn, w_ref, bias_ref, lg_ref, lb_ref, o_ref, ext_ref, sh_ref, conv_ref):
    def glu(v, g):
        return v[...].astype(F32) * jax.nn.sigmoid(g[...].astype(F32))

    _fill_ext(ext_ref, glu(vp, gp), glu(vc, gc), glu(vn, gn))
    tr, c = conv_ref.shape
    n_sh = ext_ref.shape[0] - SUBLANES
    for s in range(SUBLANES):
        sh_ref[s] = ext_ref[s:s + n_sh, :]
    first = HALO - CONF_WIDTH // 2
    for c0 in range(0, c, LANES):
        for r0 in range(0, tr, 64):
            acc = None
            for k in range(CONF_WIDTH):
                a, s = divmod(first + k, SUBLANES)
                term = w_ref[k:k + 1, c0:c0 + LANES] * sh_ref[s, SUBLANES * a + r0:SUBLANES * a + r0 + 64, c0:c0 + LANES]
                acc = term if acc is None else acc + term
            conv_ref[r0:r0 + 64, c0:c0 + LANES] = acc
    for r0 in range(0, tr, 16):
        y = _ln_rows(conv_ref[r0:r0 + 16, :] + bias_ref[...], lg_ref[...], lb_ref[...])
        o_ref[r0:r0 + 16, :] = (y * jax.nn.sigmoid(y)).astype(o_ref.dtype)


def _conformer_branch(u, w, bias, ln_g, ln_b, col0, dc, tr=256):
    b, l, _ = u.shape
    specs = _halo_specs(l, tr, dc, lambda: col0) + _halo_specs(l, tr, dc, lambda: col0 + 1)
    vec = pl.BlockSpec((1, dc), lambda bb, i: (0, 0))
    blocks = [_nbytes((tr + 2 * HALO, dc), F32)] * 2 + [_nbytes((CONF_WIDTH, dc), F32), _nbytes((tr, dc), BF16)]
    n_sh = tr + 2 * HALO - SUBLANES
    scratch = _nbytes((tr + 2 * HALO, dc), F32) + _nbytes((tr, dc), F32) + _nbytes((SUBLANES, n_sh, dc), F32)
    return pl.pallas_call(
        _conf_kernel, grid=(b, l // tr),
        in_specs=specs + [pl.BlockSpec((CONF_WIDTH, dc), lambda bb, i: (0, 0)), vec, vec, vec],
        out_specs=pl.BlockSpec((None, tr, dc), lambda bb, i: (bb, i, 0)),
        out_shape=jax.ShapeDtypeStruct((b, l, dc), BF16), name="conformer_branch",
        scratch_shapes=[pltpu.VMEM((tr + 2 * HALO, dc), F32), pltpu.VMEM((SUBLANES, n_sh, dc), F32),
                        pltpu.VMEM((tr, dc), F32)],
        compiler_params=pltpu.CompilerParams(
            dimension_semantics=("parallel", "parallel"), vmem_limit_bytes=_vmem_limit(blocks, scratch)),
    )(u, u, u, u, u, u, w, bias.reshape(1, dc), ln_g.reshape(1, dc), ln_b.reshape(1, dc))


def _sconv_kernel(gb_ref, cp, cc, cn, xp, xc, xn, w_ref, o_ref, ext_ref):
    prod = lambda a, b: a[...].astype(F32) * b[...].astype(F32)
    _fill_ext(ext_ref, prod(cp, xp), prod(cc, xc), prod(cn, xn))
    tr, c = o_ref.shape
    for c0 in range(0, c, 256):
        for r0 in range(0, tr, CONV_ROWS):
            y = _dwconv3_tile(ext_ref, w_ref, r0, CONV_ROWS, c0, 256)
            o_ref[r0:r0 + CONV_ROWS, c0:c0 + 256] = (gb_ref[r0:r0 + CONV_ROWS, c0:c0 + 256].astype(F32) * y).astype(o_ref.dtype)


def _sconv_branch(u, w, col0, ds, tr=256):
    b, l, _ = u.shape
    specs = ([pl.BlockSpec((None, tr, ds), lambda bb, i: (bb, i, col0))]
             + _halo_specs(l, tr, ds, lambda: col0 + 1) + _halo_specs(l, tr, ds, lambda: col0 + 2))
    blocks = [_nbytes((tr + 2 * HALO, ds), F32)] * 3 + [_nbytes((tr, ds), BF16)]
    return pl.pallas_call(
        _sconv_kernel, grid=(b, l // tr),
        in_specs=specs + [pl.BlockSpec((SHORT_WIDTH, ds), lambda bb, i: (0, 0))],
        out_specs=pl.BlockSpec((None, tr, ds), lambda bb, i: (bb, i, 0)),
        out_shape=jax.ShapeDtypeStruct((b, l, ds), BF16), name="sconv_branch",
        scratch_shapes=[pltpu.VMEM((tr + 2 * HALO, ds), F32)],
        compiler_params=pltpu.CompilerParams(
            dimension_semantics=("parallel", "parallel"),
            vmem_limit_bytes=_vmem_limit(blocks, _nbytes((tr + 2 * HALO, ds), F32))),
    )(u, u, u, u, u, u, u, w)


def _hyena_short_kernel(xp, xc, xn, w_ref, v_ref, x1_ref, x2_ref, ext_ref):
    _fill_ext(ext_ref, xp[...].astype(F32), xc[...].astype(F32), xn[...].astype(F32))
    tr, dh = v_ref.shape
    for part, o_ref in enumerate((v_ref, x1_ref, x2_ref)):
        for c0 in range(0, dh, 256):
            for r0 in range(0, tr, CONV_ROWS):
                o_ref[r0:r0 + CONV_ROWS, c0:c0 + 256] = _dwconv3_tile(
                    ext_ref, w_ref, r0, CONV_ROWS, part * dh + c0, 256)


def _hyena_short_conv(u, w, col0, dh, tr=256):
    b, l, _ = u.shape
    out = jax.ShapeDtypeStruct((b, l, dh), F32)
    ospec = pl.BlockSpec((None, tr, dh), lambda bb, i: (bb, i, 0))
    blocks = [_nbytes((tr + 2 * HALO, 3 * dh), F32)] + [_nbytes((tr, dh), F32)] * 3
    return pl.pallas_call(
        _hyena_short_kernel, grid=(b, l // tr),
        in_specs=(_halo_specs(l, tr, 3 * dh, lambda: col0)
                  + [pl.BlockSpec((SHORT_WIDTH, 3 * dh), lambda bb, i: (0, 0))]),
        out_specs=(ospec, ospec, ospec), out_shape=(out, out, out), name="hyena_short_conv",
        scratch_shapes=[pltpu.VMEM((tr + 2 * HALO, 3 * dh), F32)],
        compiler_params=pltpu.CompilerParams(
            dimension_semantics=("parallel", "parallel"),
            vmem_limit_bytes=_vmem_limit(blocks, _nbytes((tr + 2 * HALO, 3 * dh), F32))),
    )(u, u, u, w)


MXU_COLS = 256
GATE_ROWS = 256


def _ffn_up_gate_kernel(tiles_per_seq, ap_ref, ac_ref, an_ref, wg_ref, wv_ref, cg_ref, cv_ref, o_ref, a_ext,
                        up_ref):
    tm, tn = o_ref.shape
    rows = tm + 2 * HALO

    @pl.when(pl.program_id(1) == 0)
    def _():
        t = pl.program_id(0) % tiles_per_seq
        a_ext[0:HALO, :] = jnp.where(t > 0, ap_ref[...], jnp.zeros_like(ap_ref))
        a_ext[HALO:HALO + tm, :] = ac_ref[...]
        a_ext[HALO + tm:, :] = jnp.where(t < tiles_per_seq - 1, an_ref[...], jnp.zeros_like(an_ref))

    def conv(x, c_ref, c0):
        n = x.shape[0]
        w = c_ref[:, c0:c0 + MXU_COLS]
        y = w[0:1] * pltpu.roll(x, 1, 0) + w[1:2] * x + w[2:3] * pltpu.roll(x, n - 1, 0)
        return y[SUBLANES:n - SUBLANES]

    for s, c0 in enumerate(range(0, tn, MXU_COLS)):
        a = a_ext[...]
        up_ref[0, s] = jnp.dot(a, wg_ref[:, c0:c0 + MXU_COLS], preferred_element_type=F32)
        up_ref[1, s] = jnp.dot(a, wv_ref[:, c0:c0 + MXU_COLS], preferred_element_type=F32)
        for r0 in range(0, tm, GATE_ROWS):
            lo, hi = HALO + r0 - SUBLANES, HALO + r0 + GATE_ROWS + SUBLANES
            g = conv(up_ref[0, s, lo:hi, :], cg_ref, c0)
            v = conv(up_ref[1, s, lo:hi, :], cv_ref, c0)
            hg = 0.5 * g
            o_ref[r0:r0 + GATE_ROWS, c0:c0 + MXU_COLS] = ((hg + hg * jnp.tanh(hg)) * v).astype(o_ref.dtype)


def _ffn_up_gate(h, wu, wc, seq_len, tm=1024, tn=512):
    t, d = h.shape
    f = wu.shape[1] // 2
    nc = f // tn
    r = tm // HALO
    last = t // HALO - 1
    specs = [pl.BlockSpec((HALO, d), lambda i, j: (jnp.maximum(i * r - 1, 0), 0)),
             pl.BlockSpec((tm, d), lambda i, j: (i, 0)),
             pl.BlockSpec((HALO, d), lambda i, j: (jnp.minimum((i + 1) * r, last), 0)),
             pl.BlockSpec((d, tn), lambda i, j: (0, j)), pl.BlockSpec((d, tn), lambda i, j: (0, j + nc)),
             pl.BlockSpec((SHORT_WIDTH, tn), lambda i, j: (0, j)),
             pl.BlockSpec((SHORT_WIDTH, tn), lambda i, j: (0, j + nc))]
    blocks = [_nbytes((tm + 2 * HALO, d), BF16), 2 * _nbytes((d, tn), BF16), _nbytes((tm, tn), BF16)]
    scratch = _nbytes((tm + 2 * HALO, d), BF16) + 4 * _nbytes((tm + 2 * HALO, MXU_COLS), F32)
    return pl.pallas_call(
        functools.partial(_ffn_up_gate_kernel, seq_len // tm), grid=(t // tm, nc),
        in_specs=specs, out_specs=pl.BlockSpec((tm, tn), lambda i, j: (i, j)),
        out_shape=jax.ShapeDtypeStruct((t, f), BF16), name="ffn_up_gate",
        scratch_shapes=[pltpu.VMEM((tm + 2 * HALO, d), BF16),
                        pltpu.VMEM((2, tn // MXU_COLS, tm + 2 * HALO, MXU_COLS), F32)],
        compiler_params=pltpu.CompilerParams(
            dimension_semantics=("parallel", "arbitrary"), vmem_limit_bytes=_vmem_limit(blocks, scratch)),
    )(h, h, h, wu, wu, wc, wc)


def _dft_constants(n1):
    n = n1 * n1
    h = n1 // 2
    idx = jnp.arange(n1, dtype=jnp.int32)
    ang1 = (2.0 * math.pi / n1) * ((idx[:, None] * idx[None, :]) % n1).astype(F32)
    c1, s1 = jnp.cos(ang1), jnp.sin(ang1)
    m1 = jnp.stack([jnp.concatenate([c1[:, :h], s1[:, :h]], axis=1),
                    jnp.concatenate([-s1[:, :h], c1[:, :h]], axis=1)], axis=1).reshape(2 * n1, n1)
    m1f = jnp.stack([c1, -s1], axis=1).reshape(2 * n1, n1)
    ct, st = c1[:h, :], s1[:h, :]
    m3 = jnp.concatenate([jnp.stack([ct, -st], axis=2).reshape(h, 2 * n1),
                          jnp.stack([st, ct], axis=2).reshape(h, 2 * n1)], axis=0) / n
    k = idx[:, None, None] + n1 * idx[None, :, None]
    ang2 = (2.0 * math.pi / n) * ((k * idx[None, None, :]) % n).astype(F32)
    c2, s2 = jnp.cos(ang2), jnp.sin(ang2)
    g = jnp.concatenate([jnp.concatenate([c2, s2], axis=2), jnp.concatenate([-s2, c2], axis=2)], axis=1)
    gt = jnp.swapaxes(g, 1, 2)
    g = g.reshape(n1, 2 * n1, 2, n1).swapaxes(2, 3).reshape(n1, 2 * n1, 2 * n1)
    gt = gt.reshape(n1, 2, n1, 2 * n1).swapaxes(1, 2).reshape(n1, 2 * n1, 2 * n1)
    return tuple(m.astype(BF16) for m in (m1, m1f, g, gt, m3))


def _pack_pairs(x):
    return pltpu.bitcast(x.astype(BF16), jnp.int32)


def _unpack_pairs(x):
    return pltpu.bitcast(x, BF16)


def _gather_rows(ref, count):
    flat = ref.reshape(count * SUBLANES, LANES)
    return jnp.concatenate([flat[pl.ds(j, count, stride=SUBLANES), :] for j in range(SUBLANES)], axis=1)


def _scatter_rows(ref, val):
    count = val.shape[0]
    flat = ref.reshape(count * SUBLANES, LANES)
    for j in range(SUBLANES):
        flat[pl.ds(j, count, stride=SUBLANES), :] = val[:, j * LANES:(j + 1) * LANES]


def _slab_specs(rows):
    return [pl.BlockSpec((rows, SUBLANES, LANES), lambda ct, jb, q=q: (0, jb, ct * DFT_SLABS + q))
            for q in range(DFT_SLABS)]


def _gather_slabs(refs):
    return jnp.concatenate([_gather_rows(r, r.shape[0]) for r in refs], axis=1)


def _stage1_kernel(m_ref, *refs):
    x_refs, o_ref, slab_refs = refs[:DFT_SLABS], refs[DFT_SLABS], refs[DFT_SLABS + 1:]
    res = jnp.dot(m_ref[...], _gather_slabs(x_refs).astype(BF16), preferred_element_type=F32)
    packed = _pack_pairs(res)
    cols = SUBLANES * LANES
    for q in range(DFT_SLABS):
        _scatter_rows(slab_refs[q], packed[:, q * cols:(q + 1) * cols])
        o_ref[:, :, q * LANES:(q + 1) * LANES] = slab_refs[q][...]


def _dft_stage1(m, x, name):
    r, k = m.shape
    _, n2, c = x.shape
    wide = DFT_SLABS * LANES
    blocks = [_nbytes((k, SUBLANES, wide), F32), _nbytes((r // 2, SUBLANES, wide), F32)]
    return pl.pallas_call(
        _stage1_kernel, grid=(c // wide, n2 // SUBLANES),
        in_specs=[pl.BlockSpec((r, k), lambda ct, jb: (0, 0))] + _slab_specs(k),
        out_specs=pl.BlockSpec((r // 2, SUBLANES, wide), lambda ct, jb: (0, jb, ct)),
        out_shape=jax.ShapeDtypeStruct((r // 2, n2, c), jnp.int32), name=name,
        scratch_shapes=[pltpu.VMEM((r // 2, SUBLANES, LANES), jnp.int32)] * DFT_SLABS,
        compiler_params=pltpu.CompilerParams(
            dimension_semantics=("parallel", "parallel"), vmem_limit_bytes=_vmem_limit(blocks, 6 * blocks[1])),
    )(m, *([x] * DFT_SLABS))


def _stage2_kernel(g_ref, gt_ref, z_ref, t_ref, o_ref):
    nk, n1, _ = z_ref.shape
    for k in range(nk):
        g = g_ref[k]
        x = jnp.dot(g, _unpack_pairs(z_ref[k]), preferred_element_type=F32)
        h = jnp.dot(g, _unpack_pairs(t_ref[k]), preferred_element_type=F32)
        xr, xi, hr, hi = x[:n1], x[n1:], h[:n1], h[n1:]
        y = jnp.concatenate([xr * hr - xi * hi, xr * hi + xi * hr], axis=0).astype(BF16)
        o_ref[k] = _pack_pairs(jnp.dot(gt_ref[k], y, preferred_element_type=F32))


def _dft_stage2(g, gt, z, tf, order, nk=4):
    n1, _, c = z.shape
    blocks = [_nbytes((nk, 2 * n1, 2 * n1), BF16)] * 2 + [_nbytes((nk, n1, c), jnp.int32)] * 3
    return pl.pallas_call(
        _stage2_kernel, grid=(n1 // nk,),
        in_specs=[pl.BlockSpec((nk, 2 * n1, 2 * n1), lambda k: (k, 0, 0)),
                  pl.BlockSpec((nk, 2 * n1, 2 * n1), lambda k: (k, 0, 0)),
                  pl.BlockSpec((nk, n1, c), lambda k: (k, 0, 0)),
                  pl.BlockSpec((nk, n1, c), lambda k: (k, 0, order))],
        out_specs=pl.BlockSpec((nk, n1, c), lambda k: (k, 0, 0)),
        out_shape=jax.ShapeDtypeStruct((n1, n1, c), jnp.int32), name="dft_stage2",
        compiler_params=pltpu.CompilerParams(
            dimension_semantics=("parallel",), vmem_limit_bytes=_vmem_limit(blocks, 6 * blocks[3])),
    )(g, gt, z, tf)


def _stage3_kernel(m_ref, *refs):
    b_refs = refs[:DFT_SLABS]
    u_ref, x_ref, d_ref, o_ref = refs[DFT_SLABS:DFT_SLABS + 4]
    y_refs = refs[DFT_SLABS + 4:]
    y = jnp.dot(m_ref[...], _unpack_pairs(_gather_slabs(b_refs)), preferred_element_type=F32)
    cols = SUBLANES * LANES
    for q in range(DFT_SLABS):
        _scatter_rows(y_refs[q], y[:, q * cols:(q + 1) * cols])
        lanes = slice(q * LANES, (q + 1) * LANES)
        o_ref[:, :, lanes] = x_ref[:, :, lanes] * (y_refs[q][...] + u_ref[:, :, lanes] * d_ref[:, :, lanes])


def _dft_stage3(m3, bz, u, xg, d):
    r, k = m3.shape
    _, n2, c = u.shape
    wide = DFT_SLABS * LANES
    tile = pl.BlockSpec((r, SUBLANES, wide), lambda ct, jb: (0, jb, ct))
    blocks = [_nbytes((k // 2, SUBLANES, wide), F32)] + [_nbytes((r, SUBLANES, wide), F32)] * 3
    return pl.pallas_call(
        _stage3_kernel, grid=(c // wide, n2 // SUBLANES),
        in_specs=([pl.BlockSpec((r, k), lambda ct, jb: (0, 0))] + _slab_specs(k // 2)
                  + [tile, tile, pl.BlockSpec((1, 1, wide), lambda ct, jb: (0, 0, ct))]),
        out_specs=tile, out_shape=jax.ShapeDtypeStruct((r, n2, c), F32), name="dft_stage3",
        scratch_shapes=[pltpu.VMEM((r, SUBLANES, LANES), F32)] * DFT_SLABS,
        compiler_params=pltpu.CompilerParams(
            dimension_semantics=("parallel", "parallel"), vmem_limit_bytes=_vmem_limit(blocks, 6 * blocks[0])),
    )(m3, *([bz] * DFT_SLABS), u, xg, d.reshape(1, 1, c))


def _long_conv_gate(consts, u, xg, tf, order, d):
    m1, _, g, gt, m3 = consts
    b, l, c = u.shape
    n1 = m1.shape[1]
    u3 = u.reshape(n1, n1, c)
    z = _dft_stage1(m1, u3, "dft_stage1")
    bz = _dft_stage2(g, gt, z, tf, order)
    return _dft_stage3(m3, bz, u3, xg.reshape(n1, n1, c), d).reshape(b, l, c)


def _filter_mlp_kernel(seq_len, z_ref, t_ref, w1_ref, b1_ref, w2_ref, b2_ref, fr_ref, w3_ref, dec_ref, o_ref):
    hp = lax.Precision.HIGHEST
    fr = fr_ref[...]
    h = jnp.sin(fr * (jnp.dot(z_ref[...], w1_ref[...], precision=hp, preferred_element_type=F32) + b1_ref[...]))
    h = jnp.sin(fr * (jnp.dot(h, w2_ref[...], precision=hp, preferred_element_type=F32) + b2_ref[...]))
    h = jnp.dot(h.astype(BF16), w3_ref[...].astype(BF16), preferred_element_type=F32)
    h = h * jnp.exp(-t_ref[...] * dec_ref[...])
    tf = o_ref.shape[0]
    row = pl.program_id(0) * tf + lax.broadcasted_iota(jnp.int32, (tf, 1), 0)
    o_ref[...] = jnp.where(row == seq_len, 0.0, h).astype(o_ref.dtype)


def _hyena_taps(z_ext, t_ext, w1, b1, w2, b2, w3, freq, decay, tf=512):
    n2l, e = z_ext.shape
    seq_len = n2l // 2
    hid = w1.shape[1]
    order, _, c = decay.shape
    w3d = w3.reshape(hid, order, 2, c).transpose(2, 0, 1, 3).reshape(2, hid, order * c)
    dec = jnp.abs(decay).transpose(1, 0, 2).reshape(2, 1, order * c)
    half = seq_len // tf
    full = lambda shape: pl.BlockSpec(shape, lambda i: (0,) * len(shape))
    blocks = [_nbytes((tf, 128), F32) * 2, _nbytes((hid, order * c), F32), _nbytes((tf, order * c), F32)]
    return pl.pallas_call(
        functools.partial(_filter_mlp_kernel, seq_len), grid=(n2l // tf,),
        in_specs=[pl.BlockSpec((tf, e), lambda i: (i, 0)), pl.BlockSpec((tf, 1), lambda i: (i, 0)),
                  full((e, hid)), full((1, hid)), full((hid, hid)), full((1, hid)), full((1, hid)),
                  pl.BlockSpec((None, hid, order * c), lambda i: (i // half, 0, 0)),
                  pl.BlockSpec((None, 1, order * c), lambda i: (i // half, 0, 0))],
        out_specs=pl.BlockSpec((tf, order * c), lambda i: (i, 0)),
        out_shape=jax.ShapeDtypeStruct((n2l, order * c), F32), name="filter_taps",
        compiler_params=pltpu.CompilerParams(
            dimension_semantics=("parallel",),
            vmem_limit_bytes=_vmem_limit(blocks, 3 * _nbytes((tf, order * c), F32))),
    )(z_ext, t_ext, w1, b1.reshape(1, hid), w2, b2.reshape(1, hid), freq.reshape(1, hid), w3d, dec)


def _pos_features(seq_len, emb_dim):
    t = jnp.linspace(0.0, 1.0, seq_len, dtype=F32)[:, None]
    bands = (emb_dim - 1) // 2
    w = 2.0 * math.pi * jnp.arange(seq_len, dtype=F32)[:, None] / seq_len
    f = jnp.linspace(1e-4, bands - 1, bands, dtype=F32)[None, :]
    z = jnp.concatenate([t, jnp.cos(f * w), -jnp.sin(f * w)], axis=-1)
    pad = (-emb_dim) % 8
    z = jnp.pad(z, ((0, 0), (0, pad)))
    ext = lambda a: jnp.concatenate([a, a[:1], a[:0:-1]], axis=0)
    return ext(z), ext(t), pad


def kernel(x, ln_emb_g, ln_emb_b, w_in, conf_dw_w, conf_dw_b, conf_ln_g, conf_ln_b, sconv_w, hyena_short_w, hyena_w1, hyena_b1, hyena_w2, hyena_b2, hyena_w3, hyena_freq, hyena_decay, hyena_d, w_out, ln1_g, ln1_b, w_up, ffn_dw_w, w_down, ln2_g, ln2_b):
    b, l, d = x.shape
    depth = w_in.shape[0]
    dc = conf_dw_w.shape[2]
    ds = sconv_w.shape[2]
    dh = hyena_d.shape[2]
    dff = w_down.shape[1]
    dffp = -(-dff // 1024) * 1024
    alpha = (2.0 * depth) ** 0.25
    t = b * l
    n1 = math.isqrt(2 * l)
    assert b == 2 and n1 * n1 == 2 * l, "long conv packs two batch rows into one complex 2L = n1*n1 transform"

    consts = _dft_constants(n1)
    z_ext, t_ext, epad = _pos_features(l, hyena_w1.shape[1])

    h32, h16 = _layer_norm(x.reshape(t, d), ln_emb_g, ln_emb_b)
    for i in range(depth):
        taps = _hyena_taps(z_ext, t_ext, jnp.pad(hyena_w1[i], ((0, epad), (0, 0))), hyena_b1[i], hyena_w2[i],
                           hyena_b2[i], hyena_w3[i], hyena_freq[i], hyena_decay[i])
        oc = taps.shape[1]
        tf = _dft_stage1(consts[1], taps.reshape(n1, n1, oc), "filter_stage1")

        split = 2 * dc + 3 * ds
        assert split % MXU_COLS == 0
        wi = _cast_pad(w_in, i, 1, 1, MXU_COLS, "cast_w_in", rotate=split // MXU_COLS)
        u = _matmul(h16, wi, BF16, "mm_in").reshape(b, l, -1)
        a = _conformer_branch(u, conf_dw_w[i], conf_dw_b[i], conf_ln_g[i], conf_ln_b[i], 3 * dh // dc, dc)
        bb = _sconv_branch(u, sconv_w[i], (3 * dh + 2 * dc) // ds, ds)
        v, x1, x2 = _hyena_short_conv(u, hyena_short_w[i], 0, dh)
        zc = _long_conv_gate(consts, v, x1, tf, 0, hyena_d[i, 0])
        zc = _long_conv_gate(consts, zc, x2, tf, 1, hyena_d[i, 1])
        mix = _matmul_cat(a.reshape(t, dc), bb.reshape(t, ds), zc.reshape(t, dh),
                          _cast_pad(w_out, i, 1, 1, MXU_COLS, "cast_w_out"), BF16, "mm_out")
        h32, h16 = _layer_norm(mix, ln1_g[i], ln1_b[i], residual=h32, alpha=alpha)

        pad = dffp - dff
        wu = _cast_pad(w_up, i, 1, 2, MXU_COLS, "cast_w_up")
        wd = _cast_pad(w_down, i, 0, 1, MXU_COLS, "cast_w_down")
        wc = jnp.concatenate([jnp.pad(ffn_dw_w[i, :, :dff], ((0, 0), (0, pad))),
                              jnp.pad(ffn_dw_w[i, :, dff:], ((0, 0), (0, pad)))], axis=1)
        gated = _ffn_up_gate(h16, wu, wc, l)
        ffn = _matmul_ksplit(gated, wd, BF16, "mm_down", tk=dffp // 4)
        h32, h16 = _layer_norm(ffn, ln2_g[i], ln2_b[i], residual=h32, alpha=alpha)
    return h32.reshape(b, l, d)
```
